```python
import math
import jax, jax.numpy as jnp
from jax import lax
import numpy as np

D_MODEL = 1024
BATCH = 8
SEQ = 2048
DEPTH = 4
DEC_BATCH = 32
DEC_SEQ = 8
PAST_LEN = 8192
PAGE_SIZE = 128

MIX_WIDTH = D_MODEL
ATT_HEADS = 4
QK_DIM = 64
V_DIM = 2 * QK_DIM
ATT_WIDTH = ATT_HEADS * V_DIM
QK_COLS = ATT_HEADS * 2 * QK_DIM
GM_WIDTH = MIX_WIDTH - ATT_WIDTH
GM_GROUPS = 8
GM_GROUP_DIM = GM_WIDTH // GM_GROUPS
CHUNK = 128
ROT_DIM = QK_DIM // 4
ROPE_THETA = 500000.0
IN_COLS = 2 * QK_COLS + ATT_WIDTH + 2 * GM_WIDTH
D_FF = 2816
CONV_W = 3
PLE_DIM = 256
Q_BLOCK = 128
EPS = 1e-6
NEG = -1e30

kernel_name = 'hymba_diff_gmlp_convffn_step'


def rmsnorm(x, g):
    xf = x.astype(jnp.float32)
    r = lax.rsqrt(jnp.mean(xf * xf, axis=-1, keepdims=True) + EPS)
    return (xf * r).astype(x.dtype) * g


def rope_partial(x, pos):
    half = ROT_DIM // 2
    inv = ROPE_THETA ** (-jnp.arange(half, dtype=jnp.float32) * 2.0 / ROT_DIM)
    ang = pos.astype(jnp.float32)[:, None] * inv[None, :]
    cos = jnp.cos(ang)[:, None, None, :]
    sin = jnp.sin(ang)[:, None, None, :]
    xf = x[..., :ROT_DIM].astype(jnp.float32)
    x1, x2 = xf[..., :half], xf[..., half:]
    rot = jnp.concatenate([x1 * cos - x2 * sin, x2 * cos + x1 * sin], axis=-1).astype(x.dtype)
    return jnp.concatenate([rot, x[..., ROT_DIM:]], axis=-1)


def diff_lambda(lq1, lk1, lq2, lk2, layer):
    lam_init = 0.8 - 0.6 * math.exp(-0.3 * layer)
    f = jnp.float32
    lam = (jnp.exp(jnp.sum(lq1.astype(f) * lk1.astype(f)))
           - jnp.exp(jnp.sum(lq2.astype(f) * lk2.astype(f))) + lam_init)
    return lam, lam_init


def mixer_inputs(h, w_in, gm_norm_g, pos):
    B, S, _ = h.shape
    z = h @ w_in
    q = z[..., :QK_COLS].reshape(B, S, ATT_HEADS, 2, QK_DIM)
    k = z[..., QK_COLS:2 * QK_COLS].reshape(B, S, ATT_HEADS, 2, QK_DIM)
    o = 2 * QK_COLS
    v = z[..., o:o + ATT_WIDTH].reshape(B, S, ATT_HEADS, V_DIM)
    o += ATT_WIDTH
    u = z[..., o:o + GM_WIDTH]
    vg = z[..., o + GM_WIDTH:].reshape(B, S, GM_GROUPS, GM_GROUP_DIM)
    zg = rmsnorm(vg, gm_norm_g.reshape(GM_GROUPS, GM_GROUP_DIM)).reshape(B, S, GM_WIDTH)
    return rope_partial(q, pos), rope_partial(k, pos), v, u, zg


def diff_weights(s, lam):
    p = jax.nn.softmax(s, axis=-1)
    return p[:, :, 0] - lam * p[:, :, 1]


def diff_attn_prompt(q, k, v, lam):
    B, S = q.shape[:2]
    nb = S // Q_BLOCK
    scale = QK_DIM ** -0.5
    qb = q.reshape(B, nb, Q_BLOCK, ATT_HEADS, 2, QK_DIM).transpose(1, 0, 2, 3, 4, 5)
    kpos = jnp.arange(S)

    def block(args):
        qi, bi = args
        s = jnp.einsum('bqhmd,bkhmd->bhmqk', qi, k).astype(jnp.float32) * scale
        qpos = bi * Q_BLOCK + jnp.arange(Q_BLOCK)
        s = jnp.where(kpos[None, :] <= qpos[:, None], s, NEG)
        w = diff_weights(s, lam)
        return jnp.einsum('bhqk,bkhd->bqhd', w.astype(v.dtype), v)

    o = lax.map(block, (qb, jnp.arange(nb)))
    return o.transpose(1, 0, 2, 3, 4).reshape(B, S, ATT_HEADS, V_DIM)


def diff_attn_sample(q, k_new, v_new, k_past, v_past, lam):
    Q = q.shape[1]
    P = k_past.shape[1]
    scale = QK_DIM ** -0.5
    s_past = jnp.einsum('bqhmd,bkhmd->bhmqk', q, k_past).astype(jnp.float32) * scale
    s_new = jnp.einsum('bqhmd,bkhmd->bhmqk', q, k_new).astype(jnp.float32) * scale
    s_new = jnp.where(jnp.tril(jnp.ones((Q, Q), dtype=bool)), s_new, NEG)
    w = diff_weights(jnp.concatenate([s_past, s_new], axis=-1), lam).astype(v_new.dtype)
    return (jnp.einsum('bhqk,bkhd->bqhd', w[..., :P], v_past)
            + jnp.einsum('bhqk,bkhd->bqhd', w[..., P:], v_new))


def spatial_gate(u, z, w_s, b_s, L):
    B, S, _ = z.shape
    zc = z.reshape(B, S // L, L, GM_GROUPS, GM_GROUP_DIM)
    w = jnp.tril(w_s[:, :L, :L])
    mix = jnp.einsum('gts,bcsgd->bctgd', w, zc) + b_s[:, :L].T[None, None, :, :, None]
    return u * mix.reshape(B, S, GM_WIDTH)


def finish_layer(x, att, gm, p_i, conv_prev, subln_g, lam_init, w_out, norm_ffn, w_ffn_in,
                 conv_w, conv_b, w_ffn_out, norm_ple, w_ple_gate, w_ple_proj):
    B, S = x.shape[:2]
    att = (rmsnorm(att, subln_g) * (1.0 - lam_init)).reshape(B, S, ATT_WIDTH)
    x = x + jnp.concatenate([att, gm], axis=-1) @ w_out
    h = rmsnorm(x, norm_ffn) @ w_ffn_in
    a, c = h[..., :D_FF], h[..., D_FF:]
    a_ext = jnp.concatenate([conv_prev, a], axis=1)
    a_conv = conv_b + sum(conv_w[j] * a_ext[:, j:j + S] for j in range(CONV_W))
    x = x + (jax.nn.silu(a_conv) * c) @ w_ffn_out
    gate = jax.nn.sigmoid(rmsnorm(x, norm_ple) @ w_ple_gate)
    x = x + gate * (p_i @ w_ple_proj)
    return x, a_ext[:, -(CONV_W - 1):]


def setup_inputs(seed: int = 0) -> dict:
    key = jax.random.key(seed)
    ks = iter(jax.random.split(key, 40))
    f = jnp.float32
    nrm = lambda shape, s=1.0: jax.random.normal(next(ks), shape, f) * s
    n_pages = PAST_LEN // PAGE_SIZE
    n_used = DEC_BATCH * n_pages
    n_phys = (n_used * 5) // 4
    page_table = jax.random.permutation(next(ks), n_phys)[:n_used].reshape(DEC_BATCH, n_pages).astype(jnp.int32)
    gain = lambda shape: 1.0 + nrm(shape, 0.02)
    return {
        'x_prompt': nrm((BATCH, SEQ, D_MODEL)),
        'x_sample': nrm((DEC_BATCH, DEC_SEQ, D_MODEL)),
        'cache_k': nrm((DEPTH, n_phys, PAGE_SIZE, ATT_HEADS, 2, QK_DIM)),
        'cache_v': nrm((DEPTH, n_phys, PAGE_SIZE, ATT_HEADS, V_DIM)),
        'state_ffn_conv': nrm((DEPTH, DEC_BATCH, CONV_W - 1, D_FF)),
        'page_table': page_table,
        'p_prompt': nrm((DEPTH, BATCH, SEQ, PLE_DIM)),
        'p_sample': nrm((DEPTH, DEC_BATCH, DEC_SEQ, PLE_DIM)),
        'norm_mix': gain((DEPTH, D_MODEL)),
        'w_in': nrm((DEPTH, D_MODEL, IN_COLS), D_MODEL ** -0.5),
        'lambda_q1': nrm((DEPTH, QK_DIM), 0.1),
        'lambda_k1': nrm((DEPTH, QK_DIM), 0.1),
        'lambda_q2': nrm((DEPTH, QK_DIM), 0.1),
        'lambda_k2': nrm((DEPTH, QK_DIM), 0.1),
        'subln_g': gain((DEPTH, V_DIM)),
        'gm_norm_g': gain((DEPTH, GM_WIDTH)),
        'gm_w_s': nrm((DEPTH, GM_GROUPS, CHUNK, CHUNK), CHUNK ** -0.5),
        'gm_b_s': 1.0 + nrm((DEPTH, GM_GROUPS, CHUNK), 0.1),
        'w_out': nrm((DEPTH, MIX_WIDTH, D_MODEL), MIX_WIDTH ** -0.5),
        'norm_ffn': gain((DEPTH, D_MODEL)),
        'w_ffn_in': nrm((DEPTH, D_MODEL, 2 * D_FF), D_MODEL ** -0.5),
        'conv_w': nrm((DEPTH, CONV_W, D_FF), CONV_W ** -0.5),
        'conv_b': nrm((DEPTH, D_FF), 0.02),
        'w_ffn_out': nrm((DEPTH, D_FF, D_MODEL), D_FF ** -0.5),
        'norm_ple': gain((DEPTH, D_MODEL)),
        'w_ple_gate': nrm((DEPTH, D_MODEL, D_MODEL), D_MODEL ** -0.5),
        'w_ple_proj': nrm((DEPTH, PLE_DIM, D_MODEL), PLE_DIM ** -0.5),
        'norm_final': gain((D_MODEL,)),
    }


def reference(x_prompt, x_sample, cache_k, cache_v, state_ffn_conv, page_table, p_prompt, p_sample,
              norm_mix, w_in, lambda_q1, lambda_k1, lambda_q2, lambda_k2, subln_g, gm_norm_g,
              gm_w_s, gm_b_s, w_out, norm_ffn, w_ffn_in, conv_w, conv_b, w_ffn_out, norm_ple,
              w_ple_gate, w_ple_proj, norm_final):
    Bp, S = x_prompt.shape[:2]
    Bs, Q = x_sample.shape[:2]
    n_pages = page_table.shape[1]
    past_len = n_pages * PAGE_SIZE
    pos_p = jnp.arange(S)
    pos_s = past_len + jnp.arange(Q)
    xp, xs = x_prompt, x_sample
    kp_l, vp_l, cp_l, ks_l, vs_l, cs_l, zs_l = [], [], [], [], [], [], []
    for i in range(DEPTH):
        lam, lam_init = diff_lambda(lambda_q1[i], lambda_k1[i], lambda_q2[i], lambda_k2[i], i)
        tail = (subln_g[i], lam_init, w_out[i], norm_ffn[i], w_ffn_in[i], conv_w[i], conv_b[i],
                w_ffn_out[i], norm_ple[i], w_ple_gate[i], w_ple_proj[i])
        qp, kp, vp, up, zp = mixer_inputs(rmsnorm(xp, norm_mix[i]), w_in[i], gm_norm_g[i], pos_p)
        att_p = diff_attn_prompt(qp, kp, vp, lam)
        gm_p = spatial_gate(up, zp, gm_w_s[i], gm_b_s[i], CHUNK)
        conv0 = jnp.zeros((Bp, CONV_W - 1, D_FF), xp.dtype)
        xp, cp = finish_layer(xp, att_p, gm_p, p_prompt[i], conv0, *tail)
        qs, ksn, vsn, us, zs = mixer_inputs(rmsnorm(xs, norm_mix[i]), w_in[i], gm_norm_g[i], pos_s)
        k_past = cache_k[i, page_table].reshape(Bs, past_len, ATT_HEADS, 2, QK_DIM)
        v_past = cache_v[i, page_table].reshape(Bs, past_len, ATT_HEADS, V_DIM)
        att_s = diff_attn_sample(qs, ksn, vsn, k_past, v_past, lam)
        gm_s = spatial_gate(us, zs, gm_w_s[i], gm_b_s[i], Q)
        xs, cs = finish_layer(xs, att_s, gm_s, p_sample[i], state_ffn_conv[i], *tail)
        kp_l.append(kp); vp_l.append(vp); cp_l.append(cp)
        ks_l.append(ksn); vs_l.append(vsn); cs_l.append(cs); zs_l.append(zs)
    y_prompt = rmsnorm(xp, norm_final)
    y_sample = rmsnorm(xs, norm_final)
    return (y_prompt, y_sample, jnp.stack(kp_l), jnp.stack(vp_l), jnp.stack(cp_l),
            jnp.stack(ks_l), jnp.stack(vs_l), jnp.stack(cs_l), jnp.stack(zs_l))
```

```python
import functools
import math

import jax
import jax.numpy as jnp
from jax import lax
from jax.experimental import pallas as pl
from jax.experimental.pallas import tpu as pltpu

D_MODEL = 1024
DEPTH = 4
PAGE_SIZE = 128
ATT_HEADS = 4
QK_DIM = 64
V_DIM = 2 * QK_DIM
ATT_WIDTH = ATT_HEADS * V_DIM
QK_COLS = ATT_HEADS * 2 * QK_DIM
GM_WIDTH = D_MODEL - ATT_WIDTH
GM_GROUPS = 8
GM_GROUP_DIM = GM_WIDTH // GM_GROUPS
CHUNK = 128
ROT_DIM = QK_DIM // 4
ROPE_THETA = 500000.0
IN_COLS = 2 * QK_COLS + ATT_WIDTH + 2 * GM_WIDTH
D_FF = 2816
CONV_W = 3
PLE_DIM = 256
EPS = 1e-6
NEG = -1e30

LANES = 128
FF_CHUNK = 256
N_FF_CHUNKS = D_FF // FF_CHUNK
VMEM_LIMIT = 56 * 1024 * 1024

TM_MIX = 512
TM_FIN = 512
TQ = 256
PAGES_PER_STEP = 8

F32 = jnp.float32
BF16 = jnp.bfloat16


def _rms(x, g):
    r = lax.rsqrt(jnp.mean(x * x, axis=-1, keepdims=True) + EPS)
    return (x * r) * g


def _sigmoid(x):
    return 1.0 / (1.0 + jnp.exp(-x))


def _dot(a, b):
    return jnp.dot(a, b, preferred_element_type=F32)


def _dot_nt(a, b):
    return lax.dot_general(a, b, (((1,), (1,)), ((), ())), preferred_element_type=F32)


def _const_spec(shape, single=True):
    nd = len(shape)
    idx = lambda *_: (0,) * nd
    if single:
        return pl.BlockSpec(shape, idx, pipeline_mode=pl.Buffered(1))
    return pl.BlockSpec(shape, idx)


def _layer_spec(shape, layer):
    nd = len(shape)
    return pl.BlockSpec((None,) + tuple(shape), lambda *_: (layer,) + (0,) * nd,
                        pipeline_mode=pl.Buffered(1))


def _project(x_ref, g_ref, w_ref, rc_ref, ra_ref, rb_ref, gsum_ref, gng_ref):
    h = _rms(x_ref[...], g_ref[...]).astype(BF16)
    z = _dot(h, w_ref[...])
    rc, ra, rb = rc_ref[...], ra_ref[...], rb_ref[...]

    def rope(t):
        cols = []
        for j in range(QK_COLS // LANES):
            tj = t[:, j * LANES:(j + 1) * LANES]
            up = pltpu.roll(tj, LANES - ROT_DIM // 2, 1)
            dn = pltpu.roll(tj, ROT_DIM // 2, 1)
            cols.append(tj * rc + up * ra + dn * rb)
        return jnp.concatenate(cols, axis=1)

    q = rope(z[:, :QK_COLS])
    k = rope(z[:, QK_COLS:2 * QK_COLS])
    o = 2 * QK_COLS
    v = z[:, o:o + ATT_WIDTH]
    o += ATT_WIDTH
    u = z[:, o:o + GM_WIDTH]
    vg = z[:, o + GM_WIDTH:]
    ss = _dot((vg * vg).astype(BF16), gsum_ref[...]) * (1.0 / GM_GROUP_DIM)
    zg = (vg * lax.rsqrt(ss + EPS)) * gng_ref[...]
    return q, k, v, u, zg


def _prompt_mix_kernel(x_ref, g_ref, w_ref, rc_ref, ra_ref, rb_ref, gsum_ref, gng_ref,
                       wpair_ref, bias_ref, kall_in, vall_in,
                       kall_ref, vall_ref, qb_ref, kb_ref, vb_ref, gm_ref):
    del kall_in, vall_in
    q, k, v, u, zg = _project(x_ref, g_ref, w_ref, rc_ref, ra_ref, rb_ref, gsum_ref, gng_ref)
    kall_ref[...] = k
    vall_ref[...] = v
    qb_ref[...] = (q * (QK_DIM ** -0.5)).astype(BF16)
    kb_ref[...] = k.astype(BF16)
    vb_ref[...] = v.astype(BF16)

    tm = x_ref.shape[0]
    zb = zg.astype(BF16)
    lane = lax.broadcasted_iota(jnp.int32, (CHUNK, LANES), 1)
    wrow = lax.broadcasted_iota(jnp.int32, (CHUNK, 2 * CHUNK), 0)
    wcol = lax.broadcasted_iota(jnp.int32, (CHUNK, 2 * CHUNK), 1)
    tril = (wcol % CHUNK) <= wrow
    bias = bias_ref[...]
    zero = jnp.zeros((CHUNK, LANES), BF16)
    for pr in range(GM_WIDTH // LANES):
        wp = jnp.where(tril, wpair_ref[pr], jnp.zeros((), BF16))
        for c in range(tm // CHUNK):
            rows = slice(c * CHUNK, (c + 1) * CHUNK)
            cols = slice(pr * LANES, (pr + 1) * LANES)
            zp = zb[rows, cols]
            rhs = jnp.concatenate([jnp.where(lane < GM_GROUP_DIM, zp, zero),
                                   jnp.where(lane >= GM_GROUP_DIM, zp, zero)], axis=0)
            mix = _dot(wp, rhs) + bias[:, cols]
            gm_ref[rows, cols] = (u[rows, cols] * mix).astype(BF16)


def _sample_mix_kernel(x_ref, g_ref, w_ref, rc_ref, ra_ref, rb_ref, gsum_ref, gng_ref,
                       coef_ref, bias_ref,
                       q_ref, k_ref, v_ref, zs_ref, gm_ref):
    q, k, v, u, zg = _project(x_ref, g_ref, w_ref, rc_ref, ra_ref, rb_ref, gsum_ref, gng_ref)
    q_ref[...] = q * (QK_DIM ** -0.5)
    k_ref[...] = k
    v_ref[...] = v
    zs_ref[...] = zg
    m, w = zg.shape
    nq = coef_ref.shape[0]
    z3 = zg.reshape(m // nq, nq, w)
    mix = coef_ref[0][None] * z3
    for d in range(1, nq):
        mix = mix + coef_ref[d][None] * pltpu.roll(z3, d, 1)
    mix = mix + bias_ref[...][None]
    gm_ref[...] = (u * mix.reshape(m, w)).astype(BF16)


def _mix_common_specs(layer, tm, n_seq_tiles):
    row = lambda r: (r, 0)
    pos = lambda r: (r % n_seq_tiles, 0)
    return [
        pl.BlockSpec((tm, D_MODEL), row),
        _const_spec((1, D_MODEL)),
        _layer_spec((D_MODEL, IN_COLS), layer),
        pl.BlockSpec((tm, LANES), pos),
        pl.BlockSpec((tm, LANES), pos),
        pl.BlockSpec((tm, LANES), pos),
        _const_spec((GM_WIDTH, GM_WIDTH)),
        _const_spec((1, GM_WIDTH)),
    ]


def _prompt_mix(layer, x2d, g, w_in_b, rope, gsum, gng, wpair, bias, kall, vall, seq):
    m = x2d.shape[0]
    tm = TM_MIX
    n_seq_tiles = seq // tm
    row = lambda r: (r, 0)
    in_specs = _mix_common_specs(layer, tm, n_seq_tiles) + [
        _const_spec(wpair.shape),
        _const_spec(bias.shape),
        pl.BlockSpec(memory_space=pl.ANY),
        pl.BlockSpec(memory_space=pl.ANY),
    ]
    lrow = lambda r: (layer, r, 0)
    out_specs = [
        pl.BlockSpec((None, tm, QK_COLS), lrow),
        pl.BlockSpec((None, tm, ATT_WIDTH), lrow),
        pl.BlockSpec((tm, QK_COLS), row),
        pl.BlockSpec((tm, QK_COLS), row),
        pl.BlockSpec((tm, ATT_WIDTH), row),
        pl.BlockSpec((tm, GM_WIDTH), row),
    ]
    out_shape = [
        jax.ShapeDtypeStruct(kall.shape, F32),
        jax.ShapeDtypeStruct(vall.shape, F32),
        jax.ShapeDtypeStruct((m, QK_COLS), BF16),
        jax.ShapeDtypeStruct((m, QK_COLS), BF16),
        jax.ShapeDtypeStruct((m, ATT_WIDTH), BF16),
        jax.ShapeDtypeStruct((m, GM_WIDTH), BF16),
    ]
    return pl.pallas_call(
        _prompt_mix_kernel,
        grid=(m // tm,),
        in_specs=in_specs,
        out_specs=out_specs,
        out_shape=out_shape,
        input_output_aliases={10: 0, 11: 1},
        compiler_params=pltpu.CompilerParams(
            dimension_semantics=("arbitrary",), vmem_limit_bytes=VMEM_LIMIT),
        name="prompt_mix",
    )(x2d, g, w_in_b, *rope, gsum, gng, wpair, bias, kall, vall)


def _sample_mix(layer, x2d, g, w_in_b, rope, gsum, gng, coef, bias):
    m = x2d.shape[0]
    row = lambda r: (r, 0)
    in_specs = _mix_common_specs(layer, m, 1) + [
        _const_spec(coef.shape),
        _const_spec(bias.shape),
    ]
    out_specs = [pl.BlockSpec((m, QK_COLS), row)] * 5
    out_shape = [jax.ShapeDtypeStruct((m, QK_COLS), F32)] * 4 + [
        jax.ShapeDtypeStruct((m, GM_WIDTH), BF16)]
    return pl.pallas_call(
        _sample_mix_kernel,
        grid=(1,),
        in_specs=in_specs,
        out_specs=out_specs,
        out_shape=out_shape,
        compiler_params=pltpu.CompilerParams(
            dimension_semantics=("arbitrary",), vmem_limit_bytes=VMEM_LIMIT),
        name="sample_mix",
    )(x2d, g, w_in_b, *rope, gsum, gng, coef, bias)


def _diff_lambda(lq1, lk1, lq2, lk2, lam_init):
    a = jnp.exp(jnp.sum(lq1[...] * lk1[...], axis=-1, keepdims=True))
    b = jnp.exp(jnp.sum(lq2[...] * lk2[...], axis=-1, keepdims=True))
    return a - b + lam_init


def _prompt_attn_kernel(q_ref, k_ref, v_ref, lq1, lk1, lq2, lk2, sg_ref, o_ref,
                        m_ref, l_ref, acc_ref, *, lam_init):
    qi = pl.program_id(2)
    tq = q_ref.shape[0]
    q = q_ref[...]
    lane = lax.broadcasted_iota(jnp.int32, q.shape, 1)
    zero = jnp.zeros_like(q)
    qs = jnp.concatenate([jnp.where(lane < QK_DIM, q, zero),
                          jnp.where(lane >= QK_DIM, q, zero)], axis=0)
    m_ref[...] = jnp.full(m_ref.shape, NEG, F32)
    l_ref[...] = jnp.zeros(l_ref.shape, F32)
    acc_ref[...] = jnp.zeros(acc_ref.shape, F32)

    def block(j, masked):
        off = pl.multiple_of(j * tq, tq)
        s = _dot_nt(qs, k_ref[pl.ds(off, tq), :])
        if masked:
            r = lax.broadcasted_iota(jnp.int32, (tq, tq), 0)
            c = lax.broadcasted_iota(jnp.int32, (tq, tq), 1)
            keep = c <= r
            s = jnp.where(jnp.concatenate([keep, keep], axis=0), s, NEG)
        m_old = m_ref[...]
        m_new = jnp.maximum(m_old, jnp.max(s, axis=-1, keepdims=True))
        alpha = jnp.exp(m_old - m_new)
        p = jnp.exp(s - m_new)
        l_ref[...] = alpha * l_ref[...] + jnp.sum(p, axis=-1, keepdims=True)
        acc_ref[...] = alpha * acc_ref[...] + _dot(p.astype(BF16), v_ref[pl.ds(off, tq), :])
        m_ref[...] = m_new

    def body(j, carry):
        block(j, False)
        return carry

    lax.fori_loop(0, qi, body, 0)
    block(qi, True)

    o = acc_ref[...] / l_ref[...]
    lam = _diff_lambda(lq1, lk1, lq2, lk2, lam_init)
    att = o[:tq] - lam * o[tq:]
    o_ref[...] = (_rms(att, sg_ref[...]) * (1.0 - lam_init)).astype(BF16)


def _prompt_attn(qb, kb, vb, lams, sg, lam_init, batch, seq):
    m = qb.shape[0]
    nq = seq // TQ
    small = [_const_spec((1, QK_DIM), single=False)] * 4 + [_const_spec((1, V_DIM), single=False)]
    return pl.pallas_call(
        functools.partial(_prompt_attn_kernel, lam_init=lam_init),
        grid=(batch, ATT_HEADS, nq),
        in_specs=[
            pl.BlockSpec((TQ, LANES), lambda b, h, i: (b * nq + i, h)),
            pl.BlockSpec((seq, LANES), lambda b, h, i: (b, h)),
            pl.BlockSpec((seq, LANES), lambda b, h, i: (b, h)),
        ] + small,
        out_specs=pl.BlockSpec((TQ, LANES), lambda b, h, i: (b * nq + i, h)),
        out_shape=jax.ShapeDtypeStruct((m, ATT_WIDTH), BF16),
        scratch_shapes=[
            pltpu.VMEM((2 * TQ, 1), F32),
            pltpu.VMEM((2 * TQ, 1), F32),
            pltpu.VMEM((2 * TQ, V_DIM), F32),
        ],
        compiler_params=pltpu.CompilerParams(
            dimension_semantics=("arbitrary", "arbitrary", "arbitrary"),
            vmem_limit_bytes=VMEM_LIMIT),
        name="prompt_attn",
    )(qb, kb, vb, *lams, sg)


def _sample_attn_kernel(pt_ref, q_ref, kn_ref, vn_ref, lq1, lk1, lq2, lk2, sg_ref, *rest,
                        lam_init, n_tok):
    del pt_ref
    npg = PAGES_PER_STEP
    k_refs = rest[:npg]
    v_refs = rest[npg:2 * npg]
    o_ref, qbd_ref, m_ref, l_ref, acc_ref = rest[2 * npg:]
    g = pl.program_id(1)
    rows = qbd_ref.shape[0]

    @pl.when(g == 0)
    def _():
        qt = jnp.concatenate([q_ref[...]] * (rows // n_tok), axis=0)
        r = lax.broadcasted_iota(jnp.int32, qt.shape, 0)
        c = lax.broadcasted_iota(jnp.int32, qt.shape, 1)
        qbd = jnp.where(r // n_tok == c // QK_DIM, qt, 0.0)
        qbd_ref[...] = qbd.astype(BF16)
        s = _dot_nt(qbd, kn_ref[...])
        rr = lax.broadcasted_iota(jnp.int32, s.shape, 0)
        cc = lax.broadcasted_iota(jnp.int32, s.shape, 1)
        s = jnp.where(cc <= rr % n_tok, s, NEG)
        m0 = jnp.max(s, axis=-1, keepdims=True)
        p = jnp.exp(s - m0)
        m_ref[...] = m0
        l_ref[...] = jnp.sum(p, axis=-1, keepdims=True)
        acc_ref[...] = _dot(p, vn_ref[...])

    qbd = qbd_ref[...]
    s = jnp.concatenate([_dot_nt(qbd, kr[...].astype(BF16)) for kr in k_refs], axis=1)
    m_old = m_ref[...]
    m_new = jnp.maximum(m_old, jnp.max(s, axis=-1, keepdims=True))
    alpha = jnp.exp(m_old - m_new)
    p = jnp.exp(s - m_new)
    l_ref[...] = alpha * l_ref[...] + jnp.sum(p, axis=-1, keepdims=True)
    pb = p.astype(BF16)
    acc = alpha * acc_ref[...]
    for i, vr in enumerate(v_refs):
        acc = acc + _dot(pb[:, i * PAGE_SIZE:(i + 1) * PAGE_SIZE], vr[...].astype(BF16))
    acc_ref[...] = acc
    m_ref[...] = m_new

    @pl.when(g == pl.num_programs(1) - 1)
    def _():
        o = acc_ref[...] / l_ref[...]
        lam = _diff_lambda(lq1, lk1, lq2, lk2, lam_init)
        sg = sg_ref[...]
        for h in range(ATT_HEADS):
            r0 = h * 2 * n_tok
            cols = slice(h * V_DIM, (h + 1) * V_DIM)
            att = o[r0:r0 + n_tok, cols] - lam * o[r0 + n_tok:r0 + 2 * n_tok, cols]
            o_ref[:, cols] = _rms(att, sg) * (1.0 - lam_init)


def _sample_attn(layer, page_table, q, kn, vn, lams, sg, cache_k4, cache_v4, lam_init, n_tok):
    m = q.shape[0]
    batch, n_pages = page_table.shape
    npg = PAGES_PER_STEP
    rows = ATT_HEADS * 2 * n_tok
    tok = pl.BlockSpec((n_tok, QK_COLS), lambda b, g, pt: (b, 0))
    small = [pl.BlockSpec((1, QK_DIM), lambda b, g, pt: (0, 0))] * 4 + [
        pl.BlockSpec((1, V_DIM), lambda b, g, pt: (0, 0))]

    def page_spec(i):
        return pl.BlockSpec((None, None, PAGE_SIZE, QK_COLS),
                            lambda b, g, pt: (layer, pt[b, g * npg + i], 0, 0))

    grid_spec = pltpu.PrefetchScalarGridSpec(
        num_scalar_prefetch=1,
        grid=(batch, n_pages // npg),
        in_specs=[tok, tok, tok] + small + [page_spec(i) for i in range(npg)] * 2,
        out_specs=pl.BlockSpec((n_tok, ATT_WIDTH), lambda b, g, pt: (b, 0)),
        scratch_shapes=[
            pltpu.VMEM((rows, QK_COLS), BF16),
            pltpu.VMEM((rows, 1), F32),
            pltpu.VMEM((rows, 1), F32),
            pltpu.VMEM((rows, ATT_WIDTH), F32),
        ],
    )
    return pl.pallas_call(
        functools.partial(_sample_attn_kernel, lam_init=lam_init, n_tok=n_tok),
        grid_spec=grid_spec,
        out_shape=jax.ShapeDtypeStruct((m, ATT_WIDTH), F32),
        compiler_params=pltpu.CompilerParams(
            dimension_semantics=("arbitrary", "arbitrary"), vmem_limit_bytes=VMEM_LIMIT),
        name="sample_attn",
    )(page_table, q, kn, vn, *lams, sg, *([cache_k4] * npg), *([cache_v4] * npg))


def _finish_body(x_ref, att_ref, gm_ref, p_ref, wo_ref, nf_ref, w1a_ref, w1c_ref, cw_ref,
                 cb_ref, w2_ref, np_ref, wg_ref, wp_ref, nfin_ref, y_ref, acc_ref, conv_fn,
                 final):
    mix = jnp.concatenate([att_ref[...].astype(BF16), gm_ref[...]], axis=1)
    x1 = x_ref[...] + _dot(mix, wo_ref[...])
    hn = _rms(x1, nf_ref[...]).astype(BF16)
    acc_ref[...] = jnp.zeros(acc_ref.shape, F32)

    def chunk(c, carry):
        a = _dot(hn, w1a_ref[c])
        gate_in = _dot(hn, w1c_ref[c])
        cw = cw_ref[c]
        a1, a2 = conv_fn(c, a)
        ac = cb_ref[c] + cw[0:1] * a2 + cw[1:2] * a1 + cw[2:3] * a
        hid = (ac * _sigmoid(ac)) * gate_in
        acc_ref[...] += _dot(hid.astype(BF16), w2_ref[c])
        return carry

    lax.fori_loop(0, N_FF_CHUNKS, chunk, 0)
    x2 = x1 + acc_ref[...]
    gate = _sigmoid(_dot(_rms(x2, np_ref[...]).astype(BF16), wg_ref[...]))
    x3 = x2 + gate * _dot(p_ref[...].astype(BF16), wp_ref[...])
    y_ref[...] = _rms(x3, nfin_ref[...]) if final else x3


def _prompt_finish_kernel(x_ref, att_ref, gm_ref, p_ref, wo_ref, nf_ref, w1a_ref, w1c_ref,
                          cw_ref, cb_ref, w2_ref, np_ref, wg_ref, wp_ref, nfin_ref,
                          y_ref, cs_ref, acc_ref, carry_ref, *, final):
    s = pl.program_id(1)
    tm = x_ref.shape[0]

    @pl.when(s == 0)
    def _():
        carry_ref[...] = jnp.zeros(carry_ref.shape, F32)

    def conv_fn(c, a):
        prev = carry_ref[c]
        row = lax.broadcasted_iota(jnp.int32, a.shape, 0)
        a1 = jnp.where(row == 0, prev[1:2], pltpu.roll(a, 1, 0))
        a2 = jnp.where(row == 0, prev[0:1], jnp.where(row == 1, prev[1:2], pltpu.roll(a, 2, 0)))
        tail = a[tm - (CONV_W - 1):, :]
        carry_ref[c] = tail
        cs_ref[c] = tail
        return a1, a2

    _finish_body(x_ref, att_ref, gm_ref, p_ref, wo_ref, nf_ref, w1a_ref, w1c_ref, cw_ref,
                 cb_ref, w2_ref, np_ref, wg_ref, wp_ref, nfin_ref, y_ref, acc_ref, conv_fn,
                 final)


def _sample_finish_kernel(x_ref, att_ref, gm_ref, p_ref, wo_ref, nf_ref, w1a_ref, w1c_ref,
                          cw_ref, cb_ref, w2_ref, np_ref, wg_ref, wp_ref, nfin_ref, st_ref,
                          y_ref, cs_ref, acc_ref, *, final, n_tok):
    def conv_fn(c, a):
        m, w = a.shape
        a3 = a.reshape(m // n_tok, n_tok, w)
        prev = st_ref[c]
        t = lax.broadcasted_iota(jnp.int32, a3.shape, 1)
        p0, p1 = prev[:, 0:1, :], prev[:, 1:2, :]
        a1 = jnp.where(t == 0, p1, pltpu.roll(a3, 1, 1))
        a2 = jnp.where(t == 0, p0, jnp.where(t == 1, p1, pltpu.roll(a3, 2, 1)))
        cs_ref[c] = a3[:, n_tok - (CONV_W - 1):, :]
        return a1.reshape(m, w), a2.reshape(m, w)

    _finish_body(x_ref, att_ref, gm_ref, p_ref, wo_ref, nf_ref, w1a_ref, w1c_ref, cw_ref,
                 cb_ref, w2_ref, np_ref, wg_ref, wp_ref, nfin_ref, y_ref, acc_ref, conv_fn,
                 final)


def _finish_weight_specs(layer):
    return [
        _layer_spec((D_MODEL, D_MODEL), layer),
        _const_spec((1, D_MODEL)),
        _layer_spec((N_FF_CHUNKS, D_MODEL, FF_CHUNK), layer),
        _layer_spec((N_FF_CHUNKS, D_MODEL, FF_CHUNK), layer),
        _const_spec((N_FF_CHUNKS, CONV_W, FF_CHUNK)),
        _const_spec((N_FF_CHUNKS, 1, FF_CHUNK)),
        _layer_spec((N_FF_CHUNKS, FF_CHUNK, D_MODEL), layer),
        _const_spec((1, D_MODEL)),
        _layer_spec((D_MODEL, D_MODEL), layer),
        _layer_spec((PLE_DIM, D_MODEL), layer),
        _const_spec((1, D_MODEL)),
    ]


def _prompt_finish(layer, x2d, att, gm, p2d, weights, batch, seq, final):
    m = x2d.shape[0]
    tm = TM_FIN
    ns = seq // tm
    row = lambda b, s: (b * ns + s, 0)
    in_specs = [
        pl.BlockSpec((tm, D_MODEL), row),
        pl.BlockSpec((tm, ATT_WIDTH), row),
        pl.BlockSpec((tm, GM_WIDTH), row),
        pl.BlockSpec((tm, PLE_DIM), row),
    ] + _finish_weight_specs(layer)
    return pl.pallas_call(
        functools.partial(_prompt_finish_kernel, final=final),
        grid=(batch, ns),
        in_specs=in_specs,
        out_specs=[
            pl.BlockSpec((tm, D_MODEL), row),
            pl.BlockSpec((None, N_FF_CHUNKS, CONV_W - 1, FF_CHUNK), lambda b, s: (b, 0, 0, 0)),
        ],
        out_shape=[
            jax.ShapeDtypeStruct((m, D_MODEL), F32),
            jax.ShapeDtypeStruct((batch, N_FF_CHUNKS, CONV_W - 1, FF_CHUNK), F32),
        ],
        scratch_shapes=[
            pltpu.VMEM((tm, D_MODEL), F32),
            pltpu.VMEM((N_FF_CHUNKS, CONV_W - 1, FF_CHUNK), F32),
        ],
        compiler_params=pltpu.CompilerParams(
            dimension_semantics=("arbitrary", "arbitrary"), vmem_limit_bytes=VMEM_LIMIT),
        name="prompt_finish",
    )(x2d, att, gm, p2d, *weights)


def _sample_finish(layer, x2d, att, gm, p2d, weights, state, n_tok, final):
    m = x2d.shape[0]
    batch = m // n_tok
    row = lambda i: (0, 0)
    in_specs = [
        pl.BlockSpec((m, D_MODEL), row),
        pl.BlockSpec((m, ATT_WIDTH), row),
        pl.BlockSpec((m, GM_WIDTH), row),
        pl.BlockSpec((m, PLE_DIM), row),
    ] + _finish_weight_specs(layer) + [_const_spec(state.shape)]
    cs_shape = (N_FF_CHUNKS, batch, CONV_W - 1, FF_CHUNK)
    return pl.pallas_call(
        functools.partial(_sample_finish_kernel, final=final, n_tok=n_tok),
        grid=(1,),
        in_specs=in_specs,
        out_specs=[
            pl.BlockSpec((m, D_MODEL), row),
            pl.BlockSpec(cs_shape, lambda i: (0, 0, 0, 0)),
        ],
        out_shape=[
            jax.ShapeDtypeStruct((m, D_MODEL), F32),
            jax.ShapeDtypeStruct(cs_shape, F32),
        ],
        scratch_shapes=[pltpu.VMEM((m, D_MODEL), F32)],
        compiler_params=pltpu.CompilerParams(
            dimension_semantics=("arbitrary",), vmem_limit_bytes=VMEM_LIMIT),
        name="sample_finish",
    )(x2d, att, gm, p2d, *weights, state)


def _rope_tables(pos):
    half = ROT_DIM // 2
    inv = ROPE_THETA ** (-jnp.arange(half, dtype=F32) * 2.0 / ROT_DIM)
    ang = pos.astype(F32)[:, None] * inv[None, :]
    cos, sin = jnp.cos(ang), jnp.sin(ang)
    n = pos.shape[0]
    rest = QK_DIM - ROT_DIM
    rc = jnp.concatenate([cos, cos, jnp.ones((n, rest), F32)], axis=1)
    ra = jnp.concatenate([-sin, jnp.zeros((n, half + rest), F32)], axis=1)
    rb = jnp.concatenate([jnp.zeros((n, half), F32), sin, jnp.zeros((n, rest), F32)], axis=1)
    rep = LANES // QK_DIM
    return tuple(jnp.tile(t, (1, rep)) for t in (rc, ra, rb))


def _chunk_cols(w):
    d, k, _ = w.shape
    return w.reshape(d, k, N_FF_CHUNKS, FF_CHUNK).transpose(0, 2, 1, 3)


def kernel(x_prompt, x_sample, cache_k, cache_v, state_ffn_conv, page_table, p_prompt, p_sample,
           norm_mix, w_in, lambda_q1, lambda_k1, lambda_q2, lambda_k2, subln_g, gm_norm_g,
           gm_w_s, gm_b_s, w_out, norm_ffn, w_ffn_in, conv_w, conv_b, w_ffn_out, norm_ple,
           w_ple_gate, w_ple_proj, norm_final):
    bp, seq, _ = x_prompt.shape
    bs, n_tok, _ = x_sample.shape
    n_phys = cache_k.shape[1]
    n_pages = page_table.shape[1]
    past_len = n_pages * PAGE_SIZE
    mp, ms = bp * seq, bs * n_tok

    w_in_b = w_in.astype(BF16)
    w_out_b = w_out.astype(BF16)
    w1 = w_ffn_in.astype(BF16)
    w1a_b = _chunk_cols(w1[:, :, :D_FF])
    w1c_b = _chunk_cols(w1[:, :, D_FF:])
    w2_b = w_ffn_out.astype(BF16).reshape(DEPTH, N_FF_CHUNKS, FF_CHUNK, D_MODEL)
    wg_b = w_ple_gate.astype(BF16)
    wp_b = w_ple_proj.astype(BF16)
    cw_c = _chunk_cols(conv_w)
    cb_c = _chunk_cols(conv_b[:, None, :])
    st_c = state_ffn_conv.reshape(DEPTH, bs, CONV_W - 1, N_FF_CHUNKS, FF_CHUNK)
    st_c = st_c.transpose(0, 3, 1, 2, 4)

    rope_p = _rope_tables(jnp.arange(seq))
    rope_s = _rope_tables(past_len + jnp.arange(n_tok))
    rope_s = tuple(jnp.tile(t, (bs, 1)) for t in rope_s)
    gsum = jnp.kron(jnp.eye(GM_GROUPS, dtype=F32),
                    jnp.ones((GM_GROUP_DIM, GM_GROUP_DIM), F32)).astype(BF16)

    wpair = gm_w_s.reshape(DEPTH, GM_GROUPS // 2, 2, CHUNK, CHUNK).transpose(0, 1, 3, 2, 4)
    wpair = wpair.reshape(DEPTH, GM_GROUPS // 2, CHUNK, 2 * CHUNK).astype(BF16)
    bias_p = jnp.repeat(gm_b_s.transpose(0, 2, 1), GM_GROUP_DIM, axis=2)
    t_idx = jnp.arange(n_tok)[:, None]
    d_idx = jnp.arange(n_tok)[None, :]
    src = t_idx - d_idx
    w_small = jnp.tril(gm_w_s[:, :, :n_tok, :n_tok])
    coef = jnp.where(src >= 0, w_small[:, :, t_idx, jnp.maximum(src, 0)], 0.0)
    coef = jnp.repeat(coef.transpose(0, 3, 2, 1), GM_GROUP_DIM, axis=3)
    bias_s = jnp.repeat(gm_b_s[:, :, :n_tok].transpose(0, 2, 1), GM_GROUP_DIM, axis=2)

    cache_k4 = cache_k.reshape(DEPTH, n_phys, PAGE_SIZE, QK_COLS)
    cache_v4 = cache_v.reshape(DEPTH, n_phys, PAGE_SIZE, ATT_WIDTH)

    xp = x_prompt.reshape(mp, D_MODEL)
    xs = x_sample.reshape(ms, D_MODEL)
    pp = p_prompt.reshape(DEPTH, mp, PLE_DIM)
    ps = p_sample.reshape(DEPTH, ms, PLE_DIM)
    kall = jnp.zeros((DEPTH, mp, QK_COLS), F32)
    vall = jnp.zeros((DEPTH, mp, ATT_WIDTH), F32)
    nfin = norm_final[None, :]
    cp_l, ks_l, vs_l, cs_l, zs_l = [], [], [], [], []
    for i in range(DEPTH):
        lam_init = 0.8 - 0.6 * math.exp(-0.3 * i)
        final = i == DEPTH - 1
        lams = (lambda_q1[i][None], lambda_k1[i][None], lambda_q2[i][None], lambda_k2[i][None])
        sg = subln_g[i][None]
        gmix = norm_mix[i][None]
        gng = gm_norm_g[i][None]
        weights = (w_out_b, norm_ffn[i][None], w1a_b, w1c_b, cw_c[i], cb_c[i], w2_b,
                   norm_ple[i][None], wg_b, wp_b, nfin)

        kall, vall, qb, kb, vb, gm_p = _prompt_mix(
            i, xp, gmix, w_in_b, rope_p, gsum, gng, wpair[i], bias_p[i], kall, vall, seq)
        att_p = _prompt_attn(qb, kb, vb, lams, sg, lam_init, bp, seq)
        xp, cp = _prompt_finish(i, xp, att_p, gm_p, pp[i], weights, bp, seq, final)
        cp_l.append(cp.transpose(0, 2, 1, 3).reshape(bp, CONV_W - 1, D_FF))

        q_s, k_s, v_s, z_s, gm_s = _sample_mix(
            i, xs, gmix, w_in_b, rope_s, gsum, gng, coef[i], bias_s[i])
        att_s = _sample_attn(i, page_table, q_s, k_s, v_s, lams, sg, cache_k4, cache_v4,
                             lam_init, n_tok)
        xs, cs = _sample_finish(i, xs, att_s, gm_s, ps[i], weights, st_c[i], n_tok, final)
        ks_l.append(k_s)
        vs_l.append(v_s)
        zs_l.append(z_s)
        cs_l.append(cs.transpose(1, 2, 0, 3).reshape(bs, CONV_W - 1, D_FF))

    return (
        xp.reshape(bp, seq, D_MODEL),
        xs.reshape(bs, n_tok, D_MODEL),
        kall.reshape(DEPTH, bp, seq, ATT_HEADS, 2, QK_DIM),
        vall.reshape(DEPTH, bp, seq, ATT_HEADS, V_DIM),
        jnp.stack(cp_l),
        jnp.stack(ks_l).reshape(DEPTH, bs, n_tok, ATT_HEADS, 2, QK_DIM),
        jnp.stack(vs_l).reshape(DEPTH, bs, n_tok, ATT_HEADS, V_DIM),
        jnp.stack(cs_l),
        jnp.stack(zs_l).reshape(DEPTH, bs, n_tok, GM_WIDTH),
    )
```

```python
import functools
import math

import jax
import jax.numpy as jnp
from jax import lax
from jax.experimental import pallas as pl
from jax.experimental.pallas import tpu as pltpu

D_MODEL = 1024
DEPTH = 4
PAGE_SIZE = 128
ATT_HEADS = 4
QK_DIM = 64
V_DIM = 2 * QK_DIM
ATT_WIDTH = ATT_HEADS * V_DIM
QK_COLS = ATT_HEADS * 2 * QK_DIM
GM_WIDTH = D_MODEL - ATT_WIDTH
GM_GROUPS = 8
GM_GROUP_DIM = GM_WIDTH // GM_GROUPS
CHUNK = 128
ROT_DIM = QK_DIM // 4
ROPE_THETA = 500000.0
IN_COLS = 2 * QK_COLS + ATT_WIDTH + 2 * GM_WIDTH
D_FF = 2816
CONV_W = 3
PLE_DIM = 256
EPS = 1e-6
NEG = -1e30

LANES = 128
FF_CHUNK = 256
N_FF_CHUNKS = D_FF // FF_CHUNK
VMEM_LIMIT = 56 * 1024 * 1024

TM_MIX = 512
TM_FIN = 512
TQ = 256
PAGES_PER_STEP = 16

F32 = jnp.float32
BF16 = jnp.bfloat16


def _rms(x, g):
    r = lax.rsqrt(jnp.mean(x * x, axis=-1, keepdims=True) + EPS)
    return (x * r) * g


def _sigmoid(x):
    return 1.0 / (1.0 + jnp.exp(-x))


def _dot(a, b):
    return jnp.dot(a, b, preferred_element_type=F32)


def _dot_nt(a, b):
    return lax.dot_general(a, b, (((1,), (1,)), ((), ())), preferred_element_type=F32)


def _const_spec(shape, single=True):
    nd = len(shape)
    idx = lambda *_: (0,) * nd
    if single:
        return pl.BlockSpec(shape, idx, pipeline_mode=pl.Buffered(1))
    return pl.BlockSpec(shape, idx)


def _layer_spec(shape, layer):
    nd = len(shape)
    return pl.BlockSpec((None,) + tuple(shape), lambda *_: (layer,) + (0,) * nd,
                        pipeline_mode=pl.Buffered(1))


def _project(x_ref, g_ref, w_ref, rc_ref, ra_ref, rb_ref, gsum_ref, gng_ref):
    h = _rms(x_ref[...], g_ref[...]).astype(BF16)
    z = _dot(h, w_ref[...])
    rc, ra, rb = rc_ref[...], ra_ref[...], rb_ref[...]

    def rope(t):
        cols = []
        for j in range(QK_COLS // LANES):
            tj = t[:, j * LANES:(j + 1) * LANES]
            up = pltpu.roll(tj, LANES - ROT_DIM // 2, 1)
            dn = pltpu.roll(tj, ROT_DIM // 2, 1)
            cols.append(tj * rc + up * ra + dn * rb)
        return jnp.concatenate(cols, axis=1)

    q = rope(z[:, :QK_COLS])
    k = rope(z[:, QK_COLS:2 * QK_COLS])
    o = 2 * QK_COLS
    v = z[:, o:o + ATT_WIDTH]
    o += ATT_WIDTH
    u = z[:, o:o + GM_WIDTH]
    vg = z[:, o + GM_WIDTH:]
    ss = _dot((vg * vg).astype(BF16), gsum_ref[...]) * (1.0 / GM_GROUP_DIM)
    zg = (vg * lax.rsqrt(ss + EPS)) * gng_ref[...]
    return q, k, v, u, zg


def _prompt_mix_kernel(x_ref, g_ref, w_ref, rc_ref, ra_ref, rb_ref, gsum_ref, gng_ref,
                       wpair_ref, bias_ref, kall_in, vall_in,
                       kall_ref, vall_ref, qb_ref, ktb_ref, vb_ref, gm_ref):
    del kall_in, vall_in
    tm = x_ref.shape[0]
    q, k, v, u, zg = _project(x_ref, g_ref, w_ref, rc_ref, ra_ref, rb_ref, gsum_ref, gng_ref)
    kt = k.T
    kall_ref[...] = kt
    ktb_ref[...] = kt.astype(BF16)
    for h in range(ATT_HEADS):
        vall_ref[pl.ds(h, tm, stride=ATT_HEADS), :] = v[:, h * V_DIM:(h + 1) * V_DIM]
    qb_ref[...] = (q * (QK_DIM ** -0.5)).astype(BF16)
    vb_ref[...] = v.astype(BF16)

    zb = zg.astype(BF16)
    lane = lax.broadcasted_iota(jnp.int32, (CHUNK, LANES), 1)
    wrow = lax.broadcasted_iota(jnp.int32, (CHUNK, 2 * CHUNK), 0)
    wcol = lax.broadcasted_iota(jnp.int32, (CHUNK, 2 * CHUNK), 1)
    tril = (wcol % CHUNK) <= wrow
    bias = bias_ref[...]
    zero = jnp.zeros((CHUNK, LANES), BF16)
    for pr in range(GM_WIDTH // LANES):
        wp = jnp.where(tril, wpair_ref[pr], jnp.zeros((), BF16))
        for c in range(tm // CHUNK):
            rows = slice(c * CHUNK, (c + 1) * CHUNK)
            cols = slice(pr * LANES, (pr + 1) * LANES)
            zp = zb[rows, cols]
            rhs = jnp.concatenate([jnp.where(lane < GM_GROUP_DIM, zp, zero),
                                   jnp.where(lane >= GM_GROUP_DIM, zp, zero)], axis=0)
            mix = _dot(wp, rhs) + bias[:, cols]
            gm_ref[rows, cols] = (u[rows, cols] * mix).astype(BF16)


def _sample_mix_kernel(x_ref, g_ref, w_ref, rc_ref, ra_ref, rb_ref, gsum_ref, gng_ref,
                       coef_ref, bias_ref,
                       q_ref, k_ref, v_ref, zs_ref, gm_ref):
    q, k, v, u, zg = _project(x_ref, g_ref, w_ref, rc_ref, ra_ref, rb_ref, gsum_ref, gng_ref)
    q_ref[...] = q * (QK_DIM ** -0.5)
    k_ref[...] = k
    v_ref[...] = v
    zs_ref[...] = zg
    m, w = zg.shape
    nq = coef_ref.shape[0]
    z3 = zg.reshape(m // nq, nq, w)
    mix = coef_ref[0][None] * z3
    for d in range(1, nq):
        mix = mix + coef_ref[d][None] * pltpu.roll(z3, d, 1)
    mix = mix + bias_ref[...][None]
    gm_ref[...] = (u * mix.reshape(m, w)).astype(BF16)


def _mix_common_specs(layer, tm, n_seq_tiles):
    row = lambda r: (r, 0)
    pos = lambda r: (r % n_seq_tiles, 0)
    return [
        pl.BlockSpec((tm, D_MODEL), row),
        _const_spec((1, D_MODEL)),
        _layer_spec((D_MODEL, IN_COLS), layer),
        pl.BlockSpec((tm, LANES), pos),
        pl.BlockSpec((tm, LANES), pos),
        pl.BlockSpec((tm, LANES), pos),
        _const_spec((GM_WIDTH, GM_WIDTH)),
        _const_spec((1, GM_WIDTH)),
    ]


def _prompt_mix(layer, x2d, g, w_in_b, rope, gsum, gng, wpair, bias, kall, vall, seq):
    m = x2d.shape[0]
    tm = TM_MIX
    n_seq_tiles = seq // tm
    row = lambda r: (r, 0)
    in_specs = _mix_common_specs(layer, tm, n_seq_tiles) + [
        _const_spec(wpair.shape),
        _const_spec(bias.shape),
        pl.BlockSpec(memory_space=pl.ANY),
        pl.BlockSpec(memory_space=pl.ANY),
    ]
    out_specs = [
        pl.BlockSpec((None, None, QK_COLS, tm),
                     lambda r: (layer, r // n_seq_tiles, 0, r % n_seq_tiles)),
        pl.BlockSpec((None, tm * ATT_HEADS, V_DIM), lambda r: (layer, r, 0)),
        pl.BlockSpec((tm, QK_COLS), row),
        pl.BlockSpec((None, QK_COLS, tm), lambda r: (r // n_seq_tiles, 0, r % n_seq_tiles)),
        pl.BlockSpec((tm, ATT_WIDTH), row),
        pl.BlockSpec((tm, GM_WIDTH), row),
    ]
    out_shape = [
        jax.ShapeDtypeStruct(kall.shape, F32),
        jax.ShapeDtypeStruct(vall.shape, F32),
        jax.ShapeDtypeStruct((m, QK_COLS), BF16),
        jax.ShapeDtypeStruct((m // seq, QK_COLS, seq), BF16),
        jax.ShapeDtypeStruct((m, ATT_WIDTH), BF16),
        jax.ShapeDtypeStruct((m, GM_WIDTH), BF16),
    ]
    return pl.pallas_call(
        _prompt_mix_kernel,
        grid=(m // tm,),
        in_specs=in_specs,
        out_specs=out_specs,
        out_shape=out_shape,
        input_output_aliases={10: 0, 11: 1},
        compiler_params=pltpu.CompilerParams(
            dimension_semantics=("arbitrary",), vmem_limit_bytes=VMEM_LIMIT),
        name="prompt_mix",
    )(x2d, g, w_in_b, *rope, gsum, gng, wpair, bias, kall, vall)


def _sample_mix(layer, x2d, g, w_in_b, rope, gsum, gng, coef, bias):
    m = x2d.shape[0]
    row = lambda r: (r, 0)
    in_specs = _mix_common_specs(layer, m, 1) + [
        _const_spec(coef.shape),
        _const_spec(bias.shape),
    ]
    out_specs = [pl.BlockSpec((m, QK_COLS), row)] * 5
    out_shape = [jax.ShapeDtypeStruct((m, QK_COLS), F32)] * 4 + [
        jax.ShapeDtypeStruct((m, GM_WIDTH), BF16)]
    return pl.pallas_call(
        _sample_mix_kernel,
        grid=(1,),
        in_specs=in_specs,
        out_specs=out_specs,
        out_shape=out_shape,
        compiler_params=pltpu.CompilerParams(
            dimension_semantics=("arbitrary",), vmem_limit_bytes=VMEM_LIMIT),
        name="sample_mix",
    )(x2d, g, w_in_b, *rope, gsum, gng, coef, bias)


def _diff_lambda(lq1, lk1, lq2, lk2, lam_init):
    a = jnp.exp(jnp.sum(lq1[...] * lk1[...], axis=-1, keepdims=True))
    b = jnp.exp(jnp.sum(lq2[...] * lk2[...], axis=-1, keepdims=True))
    return a - b + lam_init


def _prompt_attn_kernel(q_ref, kt_ref, v_ref, lq1, lk1, lq2, lk2, sg_ref, o_ref, *, lam_init):
    seq = q_ref.shape[0]
    tq = TQ
    lam = _diff_lambda(lq1, lk1, lq2, lk2, lam_init)
    sg = sg_ref[...]
    lane = lax.broadcasted_iota(jnp.int32, (tq, LANES), 1)
    r = lax.broadcasted_iota(jnp.int32, (tq, tq), 0)
    c = lax.broadcasted_iota(jnp.int32, (tq, tq), 1)
    keep = jnp.concatenate([c <= r, c <= r], axis=0)
    zero = jnp.zeros((tq, LANES), BF16)
    for qi in range(seq // tq):
        n = qi * tq
        q = q_ref[n:n + tq, :]
        qs = jnp.concatenate([jnp.where(lane < QK_DIM, q, zero),
                              jnp.where(lane >= QK_DIM, q, zero)], axis=0)
        s_d = jnp.where(keep, _dot(qs, kt_ref[:, n:n + tq]), NEG)
        m = jnp.max(s_d, axis=-1, keepdims=True)
        if qi:
            s_f = _dot(qs, kt_ref[:, :n])
            m = jnp.maximum(m, jnp.max(s_f, axis=-1, keepdims=True))
        p_d = jnp.exp(s_d - m)
        l = jnp.sum(p_d, axis=-1, keepdims=True)
        o = _dot(p_d.astype(BF16), v_ref[n:n + tq, :])
        if qi:
            p_f = jnp.exp(s_f - m)
            l = l + jnp.sum(p_f, axis=-1, keepdims=True)
            o = o + _dot(p_f.astype(BF16), v_ref[:n, :])
        o = o / l
        att = o[:tq] - lam * o[tq:]
        o_ref[n:n + tq, :] = (_rms(att, sg) * (1.0 - lam_init)).astype(BF16)


def _prompt_attn(qb, ktb, vb, lams, sg, lam_init, batch, seq):
    m = qb.shape[0]
    small = [_const_spec((1, QK_DIM), single=False)] * 4 + [_const_spec((1, V_DIM), single=False)]
    return pl.pallas_call(
        functools.partial(_prompt_attn_kernel, lam_init=lam_init),
        grid=(batch, ATT_HEADS),
        in_specs=[
            pl.BlockSpec((seq, LANES), lambda b, h: (b, h)),
            pl.BlockSpec((None, LANES, seq), lambda b, h: (b, h, 0)),
            pl.BlockSpec((seq, LANES), lambda b, h: (b, h)),
        ] + small,
        out_specs=pl.BlockSpec((seq, LANES), lambda b, h: (b, h)),
        out_shape=jax.ShapeDtypeStruct((m, ATT_WIDTH), BF16),
        compiler_params=pltpu.CompilerParams(
            dimension_semantics=("arbitrary", "arbitrary"), vmem_limit_bytes=VMEM_LIMIT),
        name="prompt_attn",
    )(qb, ktb, vb, *lams, sg)


def _sample_attn_kernel(pt_ref, q_ref, kn_ref, vn_ref, lq1, lk1, lq2, lk2, sg_ref, *rest,
                        lam_init, n_tok):
    del pt_ref
    npg = PAGES_PER_STEP
    k_refs = rest[:npg]
    v_refs = rest[npg:2 * npg]
    o_ref, qbd_ref, m_ref, l_ref, acc_ref = rest[2 * npg:]
    g = pl.program_id(1)
    rows = qbd_ref.shape[0]
    hrows = rows // ATT_HEADS

    @pl.when(g == 0)
    def _():
        qt = jnp.concatenate([q_ref[...]] * (rows // n_tok), axis=0)
        r = lax.broadcasted_iota(jnp.int32, qt.shape, 0)
        c = lax.broadcasted_iota(jnp.int32, qt.shape, 1)
        qbd = jnp.where(r // n_tok == c // QK_DIM, qt, 0.0)
        qbd_ref[...] = qbd.astype(BF16)
        s = _dot_nt(qbd, kn_ref[...])
        rr = lax.broadcasted_iota(jnp.int32, s.shape, 0)
        cc = lax.broadcasted_iota(jnp.int32, s.shape, 1)
        s = jnp.where(cc <= rr % n_tok, s, NEG)
        m0 = jnp.max(s, axis=-1, keepdims=True)
        p = jnp.exp(s - m0)
        m_ref[...] = m0
        l_ref[...] = jnp.sum(p, axis=-1, keepdims=True)
        vn = vn_ref[...]
        acc_ref[...] = jnp.concatenate(
            [_dot(p[h * hrows:(h + 1) * hrows], vn[:, h * V_DIM:(h + 1) * V_DIM])
             for h in range(ATT_HEADS)], axis=0)

    qbd = qbd_ref[...]
    s = jnp.concatenate([_dot(qbd, kr[...].astype(BF16)) for kr in k_refs], axis=1)
    m_old = m_ref[...]
    m_new = jnp.maximum(m_old, jnp.max(s, axis=-1, keepdims=True))
    alpha = jnp.exp(m_old - m_new)
    p = jnp.exp(s - m_new)
    l_ref[...] = alpha * l_ref[...] + jnp.sum(p, axis=-1, keepdims=True)
    pb = p.astype(BF16)
    parts = []
    for h in range(ATT_HEADS):
        vh = jnp.concatenate(
            [vr[pl.ds(h, PAGE_SIZE, stride=ATT_HEADS), :].astype(BF16) for vr in v_refs], axis=0)
        parts.append(_dot(pb[h * hrows:(h + 1) * hrows], vh))
    acc_ref[...] = alpha * acc_ref[...] + jnp.concatenate(parts, axis=0)
    m_ref[...] = m_new

    @pl.when(g == pl.num_programs(1) - 1)
    def _():
        o = acc_ref[...] / l_ref[...]
        lam = _diff_lambda(lq1, lk1, lq2, lk2, lam_init)
        sg = sg_ref[...]
        for h in range(ATT_HEADS):
            r0 = h * hrows
            att = o[r0:r0 + n_tok] - lam * o[r0 + n_tok:r0 + 2 * n_tok]
            o_ref[:, h * V_DIM:(h + 1) * V_DIM] = _rms(att, sg) * (1.0 - lam_init)


def _sample_attn(layer, page_table, q, kn, vn, lams, sg, cache_kt, cache_vr, lam_init, n_tok):
    m = q.shape[0]
    batch, n_pages = page_table.shape
    npg = PAGES_PER_STEP
    rows = ATT_HEADS * 2 * n_tok
    tok = pl.BlockSpec((n_tok, QK_COLS), lambda b, g, pt: (b, 0))
    small = [pl.BlockSpec((1, QK_DIM), lambda b, g, pt: (0, 0))] * 4 + [
        pl.BlockSpec((1, V_DIM), lambda b, g, pt: (0, 0))]

    def page_spec(i, nrows):
        return pl.BlockSpec((None, None, nrows, PAGE_SIZE),
                            lambda b, g, pt: (layer, pt[b, g * npg + i], 0, 0))

    grid_spec = pltpu.PrefetchScalarGridSpec(
        num_scalar_prefetch=1,
        grid=(batch, n_pages // npg),
        in_specs=([tok, tok, tok] + small
                  + [page_spec(i, QK_COLS) for i in range(npg)]
                  + [page_spec(i, PAGE_SIZE * ATT_HEADS) for i in range(npg)]),
        out_specs=pl.BlockSpec((n_tok, ATT_WIDTH), lambda b, g, pt: (b, 0)),
        scratch_shapes=[
            pltpu.VMEM((rows, QK_COLS), BF16),
            pltpu.VMEM((rows, 1), F32),
            pltpu.VMEM((rows, 1), F32),
            pltpu.VMEM((rows, V_DIM), F32),
        ],
    )
    return pl.pallas_call(
        functools.partial(_sample_attn_kernel, lam_init=lam_init, n_tok=n_tok),
        grid_spec=grid_spec,
        out_shape=jax.ShapeDtypeStruct((m, ATT_WIDTH), F32),
        compiler_params=pltpu.CompilerParams(
            dimension_semantics=("arbitrary", "arbitrary"), vmem_limit_bytes=VMEM_LIMIT),
        name="sample_attn",
    )(page_table, q, kn, vn, *lams, sg, *([cache_kt] * npg), *([cache_vr] * npg))


def _finish_body(x_ref, att_ref, gm_ref, p_ref, wo_ref, nf_ref, w1a_ref, w1c_ref, cw_ref,
                 cb_ref, w2_ref, np_ref, wg_ref, wp_ref, nfin_ref, y_ref, acc_ref, conv_fn,
                 final):
    mix = jnp.concatenate([att_ref[...].astype(BF16), gm_ref[...]], axis=1)
    x1 = x_ref[...] + _dot(mix, wo_ref[...])
    hn = _rms(x1, nf_ref[...]).astype(BF16)
    acc_ref[...] = jnp.zeros(acc_ref.shape, F32)

    def chunk(c, carry):
        a = _dot(hn, w1a_ref[c])
        gate_in = _dot(hn, w1c_ref[c])
        cw = cw_ref[c]
        a1, a2 = conv_fn(c, a)
        ac = cb_ref[c] + cw[0:1] * a2 + cw[1:2] * a1 + cw[2:3] * a
        hid = (ac * _sigmoid(ac)) * gate_in
        acc_ref[...] += _dot(hid.astype(BF16), w2_ref[c])
        return carry

    lax.fori_loop(0, N_FF_CHUNKS, chunk, 0, unroll=True)
    x2 = x1 + acc_ref[...]
    gate = _sigmoid(_dot(_rms(x2, np_ref[...]).astype(BF16), wg_ref[...]))
    x3 = x2 + gate * _dot(p_ref[...].astype(BF16), wp_ref[...])
    y_ref[...] = _rms(x3, nfin_ref[...]) if final else x3


def _prompt_finish_kernel(x_ref, att_ref, gm_ref, p_ref, wo_ref, nf_ref, w1a_ref, w1c_ref,
                          cw_ref, cb_ref, w2_ref, np_ref, wg_ref, wp_ref, nfin_ref,
                          y_ref, cs_ref, acc_ref, carry_ref, *, final):
    s = pl.program_id(1)
    tm = x_ref.shape[0]

    @pl.when(s == 0)
    def _():
        carry_ref[...] = jnp.zeros(carry_ref.shape, F32)

    def conv_fn(c, a):
        prev = carry_ref[c]
        row = lax.broadcasted_iota(jnp.int32, a.shape, 0)
        a1 = jnp.where(row == 0, prev[1:2], pltpu.roll(a, 1, 0))
        a2 = jnp.where(row == 0, prev[0:1], jnp.where(row == 1, prev[1:2], pltpu.roll(a, 2, 0)))
        tail = a[tm - (CONV_W - 1):, :]
        carry_ref[c] = tail
        cs_ref[c] = tail
        return a1, a2

    _finish_body(x_ref, att_ref, gm_ref, p_ref, wo_ref, nf_ref, w1a_ref, w1c_ref, cw_ref,
                 cb_ref, w2_ref, np_ref, wg_ref, wp_ref, nfin_ref, y_ref, acc_ref, conv_fn,
                 final)


def _sample_finish_kernel(x_ref, att_ref, gm_ref, p_ref, wo_ref, nf_ref, w1a_ref, w1c_ref,
                          cw_ref, cb_ref, w2_ref, np_ref, wg_ref, wp_ref, nfin_ref, st_ref,
                          y_ref, cs_ref, acc_ref, *, final, n_tok):
    def conv_fn(c, a):
        m, w = a.shape
        a3 = a.reshape(m // n_tok, n_tok, w)
        prev = st_ref[c]
        t = lax.broadcasted_iota(jnp.int32, a3.shape, 1)
        p0, p1 = prev[:, 0:1, :], prev[:, 1:2, :]
        a1 = jnp.where(t == 0, p1, pltpu.roll(a3, 1, 1))
        a2 = jnp.where(t == 0, p0, jnp.where(t == 1, p1, pltpu.roll(a3, 2, 1)))
        cs_ref[c] = a3[:, n_tok - (CONV_W - 1):, :]
        return a1.reshape(m, w), a2.reshape(m, w)

    _finish_body(x_ref, att_ref, gm_ref, p_ref, wo_ref, nf_ref, w1a_ref, w1c_ref, cw_ref,
                 cb_ref, w2_ref, np_ref, wg_ref, wp_ref, nfin_ref, y_ref, acc_ref, conv_fn,
                 final)


def _finish_weight_specs(layer):
    return [
        _layer_spec((D_MODEL, D_MODEL), layer),
        _const_spec((1, D_MODEL)),
        _layer_spec((N_FF_CHUNKS, D_MODEL, FF_CHUNK), layer),
        _layer_spec((N_FF_CHUNKS, D_MODEL, FF_CHUNK), layer),
        _const_spec((N_FF_CHUNKS, CONV_W, FF_CHUNK)),
        _const_spec((N_FF_CHUNKS, 1, FF_CHUNK)),
        _layer_spec((N_FF_CHUNKS, FF_CHUNK, D_MODEL), layer),
        _const_spec((1, D_MODEL)),
        _layer_spec((D_MODEL, D_MODEL), layer),
        _layer_spec((PLE_DIM, D_MODEL), layer),
        _const_spec((1, D_MODEL)),
    ]


def _prompt_finish(layer, x2d, att, gm, p2d, weights, batch, seq, final):
    m = x2d.shape[0]
    tm = TM_FIN
    ns = seq // tm
    row = lambda b, s: (b * ns + s, 0)
    in_specs = [
        pl.BlockSpec((tm, D_MODEL), row),
        pl.BlockSpec((tm, ATT_WIDTH), row),
        pl.BlockSpec((tm, GM_WIDTH), row),
        pl.BlockSpec((tm, PLE_DIM), row),
    ] + _finish_weight_specs(layer)
    return pl.pallas_call(
        functools.partial(_prompt_finish_kernel, final=final),
        grid=(batch, ns),
        in_specs=in_specs,
        out_specs=[
            pl.BlockSpec((tm, D_MODEL), row),
            pl.BlockSpec((None, N_FF_CHUNKS, CONV_W - 1, FF_CHUNK), lambda b, s: (b, 0, 0, 0)),
        ],
        out_shape=[
            jax.ShapeDtypeStruct((m, D_MODEL), F32),
            jax.ShapeDtypeStruct((batch, N_FF_CHUNKS, CONV_W - 1, FF_CHUNK), F32),
        ],
        scratch_shapes=[
            pltpu.VMEM((tm, D_MODEL), F32),
            pltpu.VMEM((N_FF_CHUNKS, CONV_W - 1, FF_CHUNK), F32),
        ],
        compiler_params=pltpu.CompilerParams(
            dimension_semantics=("arbitrary", "arbitrary"), vmem_limit_bytes=VMEM_LIMIT),
        name="prompt_finish",
    )(x2d, att, gm, p2d, *weights)


def _sample_finish(layer, x2d, att, gm, p2d, weights, state, n_tok, final):
    m = x2d.shape[0]
    batch = m // n_tok
    row = lambda i: (0, 0)
    in_specs = [
        pl.BlockSpec((m, D_MODEL), row),
        pl.BlockSpec((m, ATT_WIDTH), row),
        pl.BlockSpec((m, GM_WIDTH), row),
        pl.BlockSpec((m, PLE_DIM), row),
    ] + _finish_weight_specs(layer) + [_const_spec(state.shape)]
    cs_shape = (N_FF_CHUNKS, batch, CONV_W - 1, FF_CHUNK)
    return pl.pallas_call(
        functools.partial(_sample_finish_kernel, final=final, n_tok=n_tok),
        grid=(1,),
        in_specs=in_specs,
        out_specs=[
            pl.BlockSpec((m, D_MODEL), row),
            pl.BlockSpec(cs_shape, lambda i: (0, 0, 0, 0)),
        ],
        out_shape=[
            jax.ShapeDtypeStruct((m, D_MODEL), F32),
            jax.ShapeDtypeStruct(cs_shape, F32),
        ],
        scratch_shapes=[pltpu.VMEM((m, D_MODEL), F32)],
        compiler_params=pltpu.CompilerParams(
            dimension_semantics=("arbitrary",), vmem_limit_bytes=VMEM_LIMIT),
        name="sample_finish",
    )(x2d, att, gm, p2d, *weights, state)


def _rope_tables(pos):
    half = ROT_DIM // 2
    inv = ROPE_THETA ** (-jnp.arange(half, dtype=F32) * 2.0 / ROT_DIM)
    ang = pos.astype(F32)[:, None] * inv[None, :]
    cos, sin = jnp.cos(ang), jnp.sin(ang)
    n = pos.shape[0]
    rest = QK_DIM - ROT_DIM
    rc = jnp.concatenate([cos, cos, jnp.ones((n, rest), F32)], axis=1)
    ra = jnp.concatenate([-sin, jnp.zeros((n, half + rest), F32)], axis=1)
    rb = jnp.concatenate([jnp.zeros((n, half), F32), sin, jnp.zeros((n, rest), F32)], axis=1)
    rep = LANES // QK_DIM
    return tuple(jnp.tile(t, (1, rep)) for t in (rc, ra, rb))


def _chunk_cols(w):
    d, k, _ = w.shape
    return w.reshape(d, k, N_FF_CHUNKS, FF_CHUNK).transpose(0, 2, 1, 3)


def kernel(x_prompt, x_sample, cache_k, cache_v, state_ffn_conv, page_table, p_prompt, p_sample,
           norm_mix, w_in, lambda_q1, lambda_k1, lambda_q2, lambda_k2, subln_g, gm_norm_g,
           gm_w_s, gm_b_s, w_out, norm_ffn, w_ffn_in, conv_w, conv_b, w_ffn_out, norm_ple,
           w_ple_gate, w_ple_proj, norm_final):
    bp, seq, _ = x_prompt.shape
    bs, n_tok, _ = x_sample.shape
    n_phys = cache_k.shape[1]
    n_pages = page_table.shape[1]
    past_len = n_pages * PAGE_SIZE
    mp, ms = bp * seq, bs * n_tok

    w_in_b = w_in.astype(BF16)
    w_out_b = w_out.astype(BF16)
    w1 = w_ffn_in.astype(BF16)
    w1a_b = _chunk_cols(w1[:, :, :D_FF])
    w1c_b = _chunk_cols(w1[:, :, D_FF:])
    w2_b = w_ffn_out.astype(BF16).reshape(DEPTH, N_FF_CHUNKS, FF_CHUNK, D_MODEL)
    wg_b = w_ple_gate.astype(BF16)
    wp_b = w_ple_proj.astype(BF16)
    cw_c = _chunk_cols(conv_w)
    cb_c = _chunk_cols(conv_b[:, None, :])
    st_c = state_ffn_conv.reshape(DEPTH, bs, CONV_W - 1, N_FF_CHUNKS, FF_CHUNK)
    st_c = st_c.transpose(0, 3, 1, 2, 4)

    rope_p = _rope_tables(jnp.arange(seq))
    rope_s = _rope_tables(past_len + jnp.arange(n_tok))
    rope_s = tuple(jnp.tile(t, (bs, 1)) for t in rope_s)
    gsum = jnp.kron(jnp.eye(GM_GROUPS, dtype=F32),
                    jnp.ones((GM_GROUP_DIM, GM_GROUP_DIM), F32)).astype(BF16)

    wpair = gm_w_s.reshape(DEPTH, GM_GROUPS // 2, 2, CHUNK, CHUNK).transpose(0, 1, 3, 2, 4)
    wpair = wpair.reshape(DEPTH, GM_GROUPS // 2, CHUNK, 2 * CHUNK).astype(BF16)
    bias_p = jnp.repeat(gm_b_s.transpose(0, 2, 1), GM_GROUP_DIM, axis=2)
    t_idx = jnp.arange(n_tok)[:, None]
    d_idx = jnp.arange(n_tok)[None, :]
    src = t_idx - d_idx
    w_small = jnp.tril(gm_w_s[:, :, :n_tok, :n_tok])
    coef = jnp.where(src >= 0, w_small[:, :, t_idx, jnp.maximum(src, 0)], 0.0)
    coef = jnp.repeat(coef.transpose(0, 3, 2, 1), GM_GROUP_DIM, axis=3)
    bias_s = jnp.repeat(gm_b_s[:, :, :n_tok].transpose(0, 2, 1), GM_GROUP_DIM, axis=2)

    cache_kt = cache_k.transpose(0, 1, 3, 4, 5, 2).reshape(DEPTH, n_phys, QK_COLS, PAGE_SIZE)
    cache_vr = cache_v.reshape(DEPTH, n_phys, PAGE_SIZE * ATT_HEADS, V_DIM)

    xp = x_prompt.reshape(mp, D_MODEL)
    xs = x_sample.reshape(ms, D_MODEL)
    pp = p_prompt.reshape(DEPTH, mp, PLE_DIM)
    ps = p_sample.reshape(DEPTH, ms, PLE_DIM)
    kall = jnp.zeros((DEPTH, bp, QK_COLS, seq), F32)
    vall = jnp.zeros((DEPTH, mp * ATT_HEADS, V_DIM), F32)
    nfin = norm_final[None, :]
    cp_l, ks_l, vs_l, cs_l, zs_l = [], [], [], [], []
    for i in range(DEPTH):
        lam_init = 0.8 - 0.6 * math.exp(-0.3 * i)
        final = i == DEPTH - 1
        lams = (lambda_q1[i][None], lambda_k1[i][None], lambda_q2[i][None], lambda_k2[i][None])
        sg = subln_g[i][None]
        gmix = norm_mix[i][None]
        gng = gm_norm_g[i][None]
        weights = (w_out_b, norm_ffn[i][None], w1a_b, w1c_b, cw_c[i], cb_c[i], w2_b,
                   norm_ple[i][None], wg_b, wp_b, nfin)

        kall, vall, qb, kb, vb, gm_p = _prompt_mix(
            i, xp, gmix, w_in_b, rope_p, gsum, gng, wpair[i], bias_p[i], kall, vall, seq)
        att_p = _prompt_attn(qb, kb, vb, lams, sg, lam_init, bp, seq)
        xp, cp = _prompt_finish(i, xp, att_p, gm_p, pp[i], weights, bp, seq, final)
        cp_l.append(cp.transpose(0, 2, 1, 3).reshape(bp, CONV_W - 1, D_FF))

        q_s, k_s, v_s, z_s, gm_s = _sample_mix(
            i, xs, gmix, w_in_b, rope_s, gsum, gng, coef[i], bias_s[i])
        att_s = _sample_attn(i, page_table, q_s, k_s, v_s, lams, sg, cache_kt, cache_vr,
                             lam_init, n_tok)
        xs, cs = _sample_finish(i, xs, att_s, gm_s, ps[i], weights, st_c[i], n_tok, final)
        ks_l.append(k_s)
        vs_l.append(v_s)
        zs_l.append(z_s)
        cs_l.append(cs.transpose(1, 2, 0, 3).reshape(bs, CONV_W - 1, D_FF))

    return (
        xp.reshape(bp, seq, D_MODEL),
        xs.reshape(bs, n_tok, D_MODEL),
        kall.reshape(DEPTH, bp, ATT_HEADS, 2, QK_DIM, seq).transpose(0, 1, 5, 2, 3, 4),
        vall.reshape(DEPTH, bp, seq, ATT_HEADS, V_DIM),
        jnp.stack(cp_l),
        jnp.stack(ks_l).reshape(DEPTH, bs, n_tok, ATT_HEADS, 2, QK_DIM),
        jnp.stack(vs_l).reshape(DEPTH, bs, n_tok, ATT_HEADS, V_DIM),
        jnp.stack(cs_l),
        jnp.stack(zs_l).reshape(DEPTH, bs, n_tok, GM_WIDTH),
    )
```

```python
import functools
import math

import jax
import jax.numpy as jnp
from jax import lax
from jax.experimental import pallas as pl
from jax.experimental.pallas import tpu as pltpu

D_MODEL = 1024
DEPTH = 4
PAGE_SIZE = 128
ATT_HEADS = 4
QK_DIM = 64
V_DIM = 2 * QK_DIM
ATT_WIDTH = ATT_HEADS * V_DIM
QK_COLS = ATT_HEADS * 2 * QK_DIM
GM_WIDTH = D_MODEL - ATT_WIDTH
GM_GROUPS = 8
GM_GROUP_DIM = GM_WIDTH // GM_GROUPS
CHUNK = 128
ROT_DIM = QK_DIM // 4
ROPE_THETA = 500000.0
IN_COLS = 2 * QK_COLS + ATT_WIDTH + 2 * GM_WIDTH
D_FF = 2816
CONV_W = 3
PLE_DIM = 256
EPS = 1e-6
NEG = -1e30

LANES = 128
FF_CHUNK = 256
N_FF_CHUNKS = D_FF // FF_CHUNK
VMEM_LIMIT = 56 * 1024 * 1024

TM_MIX = 512
TM_FIN = 512
TQ = 256
PAGES_PER_STEP = 16

F32 = jnp.float32
BF16 = jnp.bfloat16


def _rms(x, g):
    r = lax.rsqrt(jnp.mean(x * x, axis=-1, keepdims=True) + EPS)
    return (x * r) * g


def _sigmoid(x):
    return 1.0 / (1.0 + jnp.exp(-x))


def _dot(a, b):
    return jnp.dot(a, b, preferred_element_type=F32)


def _dot_nt(a, b):
    return lax.dot_general(a, b, (((1,), (1,)), ((), ())), preferred_element_type=F32)


def _const_spec(shape, single=True):
    nd = len(shape)
    idx = lambda *_: (0,) * nd
    if single:
        return pl.BlockSpec(shape, idx, pipeline_mode=pl.Buffered(1))
    return pl.BlockSpec(shape, idx)


def _layer_spec(shape, layer):
    nd = len(shape)
    return pl.BlockSpec((None,) + tuple(shape), lambda *_: (layer,) + (0,) * nd,
                        pipeline_mode=pl.Buffered(1))


def _project(x_ref, g_ref, w_ref, rc_ref, ra_ref, rb_ref, gsum_ref, gng_ref):
    h = _rms(x_ref[...], g_ref[...]).astype(BF16)
    z = _dot(h, w_ref[...])
    rc, ra, rb = rc_ref[...], ra_ref[...], rb_ref[...]

    def rope(t):
        cols = []
        for j in range(QK_COLS // LANES):
            tj = t[:, j * LANES:(j + 1) * LANES]
            up = pltpu.roll(tj, LANES - ROT_DIM // 2, 1)
            dn = pltpu.roll(tj, ROT_DIM // 2, 1)
            cols.append(tj * rc + up * ra + dn * rb)
        return jnp.concatenate(cols, axis=1)

    q = rope(z[:, :QK_COLS])
    k = rope(z[:, QK_COLS:2 * QK_COLS])
    o = 2 * QK_COLS
    v = z[:, o:o + ATT_WIDTH]
    o += ATT_WIDTH
    u = z[:, o:o + GM_WIDTH]
    vg = z[:, o + GM_WIDTH:]
    ss = _dot((vg * vg).astype(BF16), gsum_ref[...]) * (1.0 / GM_GROUP_DIM)
    zg = (vg * lax.rsqrt(ss + EPS)) * gng_ref[...]
    return q, k, v, u, zg


def _prompt_mix_kernel(x_ref, g_ref, w_ref, rc_ref, ra_ref, rb_ref, gsum_ref, gng_ref,
                       wpair_ref, bias_ref, kall_in, vall_in,
                       kall_ref, vall_ref, qb_ref, ktb_ref, vb_ref, gm_ref):
    del kall_in, vall_in
    tm = x_ref.shape[0]
    q, k, v, u, zg = _project(x_ref, g_ref, w_ref, rc_ref, ra_ref, rb_ref, gsum_ref, gng_ref)
    kt = k.T
    kall_ref[...] = kt
    ktb_ref[...] = kt.astype(BF16)
    for h in range(ATT_HEADS):
        vall_ref[pl.ds(h, tm, stride=ATT_HEADS), :] = v[:, h * V_DIM:(h + 1) * V_DIM]
    qb_ref[...] = (q * (QK_DIM ** -0.5)).astype(BF16)
    vb_ref[...] = v.astype(BF16)

    zb = zg.astype(BF16)
    lane = lax.broadcasted_iota(jnp.int32, (CHUNK, LANES), 1)
    wrow = lax.broadcasted_iota(jnp.int32, (CHUNK, 2 * CHUNK), 0)
    wcol = lax.broadcasted_iota(jnp.int32, (CHUNK, 2 * CHUNK), 1)
    tril = (wcol % CHUNK) <= wrow
    bias = bias_ref[...]
    zero = jnp.zeros((CHUNK, LANES), BF16)
    for pr in range(GM_WIDTH // LANES):
        wp = jnp.where(tril, wpair_ref[pr], jnp.zeros((), BF16))
        for c in range(tm // CHUNK):
            rows = slice(c * CHUNK, (c + 1) * CHUNK)
            cols = slice(pr * LANES, (pr + 1) * LANES)
            zp = zb[rows, cols]
            rhs = jnp.concatenate([jnp.where(lane < GM_GROUP_DIM, zp, zero),
                                   jnp.where(lane >= GM_GROUP_DIM, zp, zero)], axis=0)
            mix = _dot(wp, rhs) + bias[:, cols]
            gm_ref[rows, cols] = (u[rows, cols] * mix).astype(BF16)


def _sample_mix_kernel(x_ref, g_ref, w_ref, rc_ref, ra_ref, rb_ref, gsum_ref, gng_ref,
                       coef_ref, bias_ref,
                       q_ref, k_ref, v_ref, zs_ref, gm_ref):
    q, k, v, u, zg = _project(x_ref, g_ref, w_ref, rc_ref, ra_ref, rb_ref, gsum_ref, gng_ref)
    q_ref[...] = q * (QK_DIM ** -0.5)
    k_ref[...] = k
    v_ref[...] = v
    zs_ref[...] = zg
    m, w = zg.shape
    nq = coef_ref.shape[0]
    z3 = zg.reshape(m // nq, nq, w)
    mix = coef_ref[0][None] * z3
    for d in range(1, nq):
        mix = mix + coef_ref[d][None] * pltpu.roll(z3, d, 1)
    mix = mix + bias_ref[...][None]
    gm_ref[...] = (u * mix.reshape(m, w)).astype(BF16)


def _mix_common_specs(layer, tm, n_seq_tiles):
    row = lambda r: (r, 0)
    pos = lambda r: (r % n_seq_tiles, 0)
    return [
        pl.BlockSpec((tm, D_MODEL), row),
        _const_spec((1, D_MODEL)),
        _layer_spec((D_MODEL, IN_COLS), layer),
        pl.BlockSpec((tm, LANES), pos),
        pl.BlockSpec((tm, LANES), pos),
        pl.BlockSpec((tm, LANES), pos),
        _const_spec((GM_WIDTH, GM_WIDTH)),
        _const_spec((1, GM_WIDTH)),
    ]


def _prompt_mix(layer, x2d, g, w_in_b, rope, gsum, gng, wpair, bias, kall, vall, seq):
    m = x2d.shape[0]
    tm = TM_MIX
    n_seq_tiles = seq // tm
    row = lambda r: (r, 0)
    in_specs = _mix_common_specs(layer, tm, n_seq_tiles) + [
        _const_spec(wpair.shape),
        _const_spec(bias.shape),
        pl.BlockSpec(memory_space=pl.ANY),
        pl.BlockSpec(memory_space=pl.ANY),
    ]
    out_specs = [
        pl.BlockSpec((None, None, QK_COLS, tm),
                     lambda r: (layer, r // n_seq_tiles, 0, r % n_seq_tiles)),
        pl.BlockSpec((None, tm * ATT_HEADS, V_DIM), lambda r: (layer, r, 0)),
        pl.BlockSpec((tm, QK_COLS), row),
        pl.BlockSpec((None, QK_COLS, tm), lambda r: (r // n_seq_tiles, 0, r % n_seq_tiles)),
        pl.BlockSpec((tm, ATT_WIDTH), row),
        pl.BlockSpec((tm, GM_WIDTH), row),
    ]
    out_shape = [
        jax.ShapeDtypeStruct(kall.shape, F32),
        jax.ShapeDtypeStruct(vall.shape, F32),
        jax.ShapeDtypeStruct((m, QK_COLS), BF16),
        jax.ShapeDtypeStruct((m // seq, QK_COLS, seq), BF16),
        jax.ShapeDtypeStruct((m, ATT_WIDTH), BF16),
        jax.ShapeDtypeStruct((m, GM_WIDTH), BF16),
    ]
    return pl.pallas_call(
        _prompt_mix_kernel,
        grid=(m // tm,),
        in_specs=in_specs,
        out_specs=out_specs,
        out_shape=out_shape,
        input_output_aliases={10: 0, 11: 1},
        compiler_params=pltpu.CompilerParams(
            dimension_semantics=("arbitrary",), vmem_limit_bytes=VMEM_LIMIT),
        name="prompt_mix",
    )(x2d, g, w_in_b, *rope, gsum, gng, wpair, bias, kall, vall)


def _sample_mix(layer, x2d, g, w_in_b, rope, gsum, gng, coef, bias):
    m = x2d.shape[0]
    row = lambda r: (r, 0)
    in_specs = _mix_common_specs(layer, m, 1) + [
        _const_spec(coef.shape),
        _const_spec(bias.shape),
    ]
    out_specs = [pl.BlockSpec((m, QK_COLS), row)] * 5
    out_shape = [jax.ShapeDtypeStruct((m, QK_COLS), F32)] * 4 + [
        jax.ShapeDtypeStruct((m, GM_WIDTH), BF16)]
    return pl.pallas_call(
        _sample_mix_kernel,
        grid=(1,),
        in_specs=in_specs,
        out_specs=out_specs,
        out_shape=out_shape,
        compiler_params=pltpu.CompilerParams(
            dimension_semantics=("arbitrary",), vmem_limit_bytes=VMEM_LIMIT),
        name="sample_mix",
    )(x2d, g, w_in_b, *rope, gsum, gng, coef, bias)


def _diff_lambda(lq1, lk1, lq2, lk2, lam_init):
    a = jnp.exp(jnp.sum(lq1[...] * lk1[...], axis=-1, keepdims=True))
    b = jnp.exp(jnp.sum(lq2[...] * lk2[...], axis=-1, keepdims=True))
    return a - b + lam_init


def _prompt_attn_kernel(q_ref, kt_ref, v_ref, lq1, lk1, lq2, lk2, sg_ref, o_ref, *, lam_init):
    seq = q_ref.shape[0]
    tq = TQ
    lam = _diff_lambda(lq1, lk1, lq2, lk2, lam_init)
    sg = sg_ref[...]
    lane = lax.broadcasted_iota(jnp.int32, (tq, LANES), 1)
    r = lax.broadcasted_iota(jnp.int32, (tq, tq), 0)
    c = lax.broadcasted_iota(jnp.int32, (tq, tq), 1)
    keep = jnp.concatenate([c <= r, c <= r], axis=0)
    zero = jnp.zeros((tq, LANES), BF16)
    for qi in range(seq // tq):
        n = qi * tq
        q = q_ref[n:n + tq, :]
        qs = jnp.concatenate([jnp.where(lane < QK_DIM, q, zero),
                              jnp.where(lane >= QK_DIM, q, zero)], axis=0)
        s_d = jnp.where(keep, _dot(qs, kt_ref[:, n:n + tq]), NEG)
        m = jnp.max(s_d, axis=-1, keepdims=True)
        if qi:
            s_f = _dot(qs, kt_ref[:, :n])
            m = jnp.maximum(m, jnp.max(s_f, axis=-1, keepdims=True))
        p_d = jnp.exp(s_d - m)
        l = jnp.sum(p_d, axis=-1, keepdims=True)
        o = _dot(p_d.astype(BF16), v_ref[n:n + tq, :])
        if qi:
            p_f = jnp.exp(s_f - m)
            l = l + jnp.sum(p_f, axis=-1, keepdims=True)
            o = o + _dot(p_f.astype(BF16), v_ref[:n, :])
        o = o / l
        att = o[:tq] - lam * o[tq:]
        o_ref[n:n + tq, :] = (_rms(att, sg) * (1.0 - lam_init)).astype(BF16)


def _prompt_attn(qb, ktb, vb, lams, sg, lam_init, batch, seq):
    m = qb.shape[0]
    small = [_const_spec((1, QK_DIM), single=False)] * 4 + [_const_spec((1, V_DIM), single=False)]
    return pl.pallas_call(
        functools.partial(_prompt_attn_kernel, lam_init=lam_init),
        grid=(batch, ATT_HEADS),
        in_specs=[
            pl.BlockSpec((seq, LANES), lambda b, h: (b, h)),
            pl.BlockSpec((None, LANES, seq), lambda b, h: (b, h, 0)),
            pl.BlockSpec((seq, LANES), lambda b, h: (b, h)),
        ] + small,
        out_specs=pl.BlockSpec((seq, LANES), lambda b, h: (b, h)),
        out_shape=jax.ShapeDtypeStruct((m, ATT_WIDTH), BF16),
        compiler_params=pltpu.CompilerParams(
            dimension_semantics=("arbitrary", "arbitrary"), vmem_limit_bytes=VMEM_LIMIT),
        name="prompt_attn",
    )(qb, ktb, vb, *lams, sg)


def _sample_attn_step(first, last, q_ref, kn_ref, vn_ref, lam_refs, sg_ref, k_refs, v_refs,
                      o_ref, qbd_ref, m_ref, l_ref, acc_ref, lam_init, n_tok):
    rows = qbd_ref.shape[0]
    hrows = rows // ATT_HEADS

    if first:
        qt = jnp.concatenate([q_ref[...]] * (rows // n_tok), axis=0)
        r = lax.broadcasted_iota(jnp.int32, qt.shape, 0)
        c = lax.broadcasted_iota(jnp.int32, qt.shape, 1)
        qbd = jnp.where(r // n_tok == c // QK_DIM, qt, 0.0)
        qbd_ref[...] = qbd.astype(BF16)
        s = _dot_nt(qbd, kn_ref[...])
        rr = lax.broadcasted_iota(jnp.int32, s.shape, 0)
        cc = lax.broadcasted_iota(jnp.int32, s.shape, 1)
        s = jnp.where(cc <= rr % n_tok, s, NEG)
        m0 = jnp.max(s, axis=-1, keepdims=True)
        p = jnp.exp(s - m0)
        m_ref[...] = m0
        l_ref[...] = jnp.sum(p, axis=-1, keepdims=True)
        vn = vn_ref[...]
        acc_ref[...] = jnp.concatenate(
            [_dot(p[h * hrows:(h + 1) * hrows], vn[:, h * V_DIM:(h + 1) * V_DIM])
             for h in range(ATT_HEADS)], axis=0)

    qbd = qbd_ref[...]
    s = jnp.concatenate([_dot(qbd, kr[...].astype(BF16)) for kr in k_refs], axis=1)
    m_old = m_ref[...]
    m_new = jnp.maximum(m_old, jnp.max(s, axis=-1, keepdims=True))
    alpha = jnp.exp(m_old - m_new)
    p = jnp.exp(s - m_new)
    l_ref[...] = alpha * l_ref[...] + jnp.sum(p, axis=-1, keepdims=True)
    pb = p.astype(BF16)
    parts = []
    for h in range(ATT_HEADS):
        vh = jnp.concatenate(
            [vr[pl.ds(h, PAGE_SIZE, stride=ATT_HEADS), :].astype(BF16) for vr in v_refs], axis=0)
        parts.append(_dot(pb[h * hrows:(h + 1) * hrows], vh))
    acc_ref[...] = alpha * acc_ref[...] + jnp.concatenate(parts, axis=0)
    m_ref[...] = m_new

    if last:
        o = acc_ref[...] / l_ref[...]
        lam = _diff_lambda(*lam_refs, lam_init)
        sg = sg_ref[...]
        for h in range(ATT_HEADS):
            r0 = h * hrows
            att = o[r0:r0 + n_tok] - lam * o[r0 + n_tok:r0 + 2 * n_tok]
            o_ref[:, h * V_DIM:(h + 1) * V_DIM] = _rms(att, sg) * (1.0 - lam_init)


def _ffn_pre(x_ref, att_ref, gm_ref, wo_ref, nf_ref, x1_ref, hn_ref, acc_ref):
    mix = jnp.concatenate([att_ref[...].astype(BF16), gm_ref[...]], axis=1)
    x1 = x_ref[...] + _dot(mix, wo_ref[...])
    x1_ref[...] = x1
    hn_ref[...] = _rms(x1, nf_ref[...]).astype(BF16)
    acc_ref[...] = jnp.zeros(acc_ref.shape, F32)


def _ffn_chunk(c, hn_ref, w1_ref, cw_ref, cb_ref, w2_ref, acc_ref, conv_fn):
    cols = slice(c * FF_CHUNK, (c + 1) * FF_CHUNK)
    gcols = slice(D_FF + c * FF_CHUNK, D_FF + (c + 1) * FF_CHUNK)
    hn = hn_ref[...]
    a = _dot(hn, w1_ref[:, cols])
    gate_in = _dot(hn, w1_ref[:, gcols])
    cw = cw_ref[:, cols]
    a1, a2 = conv_fn(cols, a)
    ac = cb_ref[:, cols] + cw[0:1] * a2 + cw[1:2] * a1 + cw[2:3] * a
    hid = (ac * _sigmoid(ac)) * gate_in
    acc_ref[...] += _dot(hid.astype(BF16), w2_ref[cols, :])


def _ffn_post(x1_ref, acc_ref, p_ref, np_ref, wg_ref, wp_ref, nfin_ref, y_ref, final):
    x2 = x1_ref[...] + acc_ref[...]
    gate = _sigmoid(_dot(_rms(x2, np_ref[...]).astype(BF16), wg_ref[...]))
    x3 = x2 + gate * _dot(p_ref[...].astype(BF16), wp_ref[...])
    y_ref[...] = _rms(x3, nfin_ref[...]) if final else x3


def _split_chunks(n_parts):
    base, extra = divmod(N_FF_CHUNKS, n_parts)
    out, start = [], 0
    for i in range(n_parts):
        size = base + (1 if i < extra else 0)
        out.append(range(start, start + size))
        start += size
    return out


def _prompt_finish_kernel(pt_ref, x_ref, att_ref, gm_ref, p_ref, wo_ref, nf_ref, w1_ref, cw_ref,
                          cb_ref, w2_ref, np_ref, wg_ref, wp_ref, nfin_ref,
                          q_ref, kn_ref, vn_ref, lq1, lk1, lq2, lk2, sg_ref, *rest,
                          final, lam_init, n_tok, n_sub):
    del pt_ref
    npg = PAGES_PER_STEP
    k_refs = rest[:npg]
    v_refs = rest[npg:2 * npg]
    (y_ref, cs_ref, atts_ref,
     hn_ref, acc_ref, carry_ref, qbd_ref, m_ref, l_ref, accs_ref) = rest[2 * npg:]
    x1_ref = y_ref
    s = pl.program_id(1)
    k = pl.program_id(2)
    tm = x_ref.shape[0]

    def conv_fn(cols, a):
        prev = carry_ref[:, cols]
        row = lax.broadcasted_iota(jnp.int32, a.shape, 0)
        a1 = jnp.where(row == 0, prev[1:2], pltpu.roll(a, 1, 0))
        a2 = jnp.where(row == 0, prev[0:1], jnp.where(row == 1, prev[1:2], pltpu.roll(a, 2, 0)))
        tail = a[tm - (CONV_W - 1):, :]
        carry_ref[:, cols] = tail
        cs_ref[:, cols] = tail
        return a1, a2

    @pl.when(jnp.logical_and(s == 0, k == 0))
    def _():
        carry_ref[...] = jnp.zeros(carry_ref.shape, F32)

    for sub, chunks in enumerate(_split_chunks(n_sub)):
        @pl.when(k == sub)
        def _(sub=sub, chunks=chunks):
            _sample_attn_step(sub == 0, sub == n_sub - 1, q_ref, kn_ref, vn_ref,
                              (lq1, lk1, lq2, lk2), sg_ref, k_refs, v_refs, atts_ref,
                              qbd_ref, m_ref, l_ref, accs_ref, lam_init, n_tok)
            if sub == 0:
                _ffn_pre(x_ref, att_ref, gm_ref, wo_ref, nf_ref, x1_ref, hn_ref, acc_ref)
            for c in chunks:
                _ffn_chunk(c, hn_ref, w1_ref, cw_ref, cb_ref, w2_ref, acc_ref, conv_fn)
            if sub == n_sub - 1:
                _ffn_post(x1_ref, acc_ref, p_ref, np_ref, wg_ref, wp_ref, nfin_ref, y_ref, final)


def _sample_finish_kernel(x_ref, att_ref, gm_ref, p_ref, wo_ref, nf_ref, w1_ref, cw_ref,
                          cb_ref, w2_ref, np_ref, wg_ref, wp_ref, nfin_ref, st_ref,
                          y_ref, cs_ref, hn_ref, acc_ref, *, final, n_tok):
    x1_ref = y_ref

    def conv_fn(cols, a):
        m, w = a.shape
        a3 = a.reshape(m // n_tok, n_tok, w)
        prev = st_ref[:, :, cols]
        t = lax.broadcasted_iota(jnp.int32, a3.shape, 1)
        p0, p1 = prev[:, 0:1, :], prev[:, 1:2, :]
        a1 = jnp.where(t == 0, p1, pltpu.roll(a3, 1, 1))
        a2 = jnp.where(t == 0, p0, jnp.where(t == 1, p1, pltpu.roll(a3, 2, 1)))
        cs_ref[:, :, cols] = a3[:, n_tok - (CONV_W - 1):, :]
        return a1.reshape(m, w), a2.reshape(m, w)

    _ffn_pre(x_ref, att_ref, gm_ref, wo_ref, nf_ref, x1_ref, hn_ref, acc_ref)
    for c in range(N_FF_CHUNKS):
        _ffn_chunk(c, hn_ref, w1_ref, cw_ref, cb_ref, w2_ref, acc_ref, conv_fn)
    _ffn_post(x1_ref, acc_ref, p_ref, np_ref, wg_ref, wp_ref, nfin_ref, y_ref, final)


def _finish_weight_specs(layer):
    return [
        _layer_spec((D_MODEL, D_MODEL), layer),
        _const_spec((1, D_MODEL)),
        _layer_spec((D_MODEL, 2 * D_FF), layer),
        _const_spec((CONV_W, D_FF)),
        _const_spec((1, D_FF)),
        _layer_spec((D_FF, D_MODEL), layer),
        _const_spec((1, D_MODEL)),
        _layer_spec((D_MODEL, D_MODEL), layer),
        _layer_spec((PLE_DIM, D_MODEL), layer),
        _const_spec((1, D_MODEL)),
    ]


def _finish_scratch(tm):
    return [
        pltpu.VMEM((tm, D_MODEL), BF16),
        pltpu.VMEM((tm, D_MODEL), F32),
    ]


def _prompt_finish(layer, x2d, att, gm, p2d, weights, batch, seq, final,
                   page_table, q_s, kn_s, vn_s, lams, sg, cache_kt, cache_vr, lam_init, n_tok):
    m = x2d.shape[0]
    tm = TM_FIN
    ns = seq // tm
    npg = PAGES_PER_STEP
    bs, n_pages = page_table.shape
    n_sub = n_pages // npg
    assert bs == batch * ns and n_pages == n_sub * npg and n_sub <= N_FF_CHUNKS
    rows = ATT_HEADS * 2 * n_tok
    row = lambda b, s, k, pt: (b * ns + s, 0)
    fixed = lambda *_: (0, 0)
    in_specs = [
        pl.BlockSpec((tm, D_MODEL), row),
        pl.BlockSpec((tm, ATT_WIDTH), row),
        pl.BlockSpec((tm, GM_WIDTH), row),
        pl.BlockSpec((tm, PLE_DIM), row),
    ] + _finish_weight_specs(layer) + [
        pl.BlockSpec((n_tok, QK_COLS), row),
        pl.BlockSpec((n_tok, QK_COLS), row),
        pl.BlockSpec((n_tok, ATT_WIDTH), row),
    ] + [pl.BlockSpec((1, QK_DIM), fixed)] * 4 + [pl.BlockSpec((1, V_DIM), fixed)]

    def page_spec(i, nrows):
        return pl.BlockSpec((None, None, nrows, PAGE_SIZE),
                            lambda b, s, k, pt: (layer, pt[b * ns + s, k * npg + i], 0, 0))

    in_specs += [page_spec(i, QK_COLS) for i in range(npg)]
    in_specs += [page_spec(i, PAGE_SIZE * ATT_HEADS) for i in range(npg)]
    grid_spec = pltpu.PrefetchScalarGridSpec(
        num_scalar_prefetch=1,
        grid=(batch, ns, n_sub),
        in_specs=in_specs,
        out_specs=[
            pl.BlockSpec((tm, D_MODEL), row),
            pl.BlockSpec((None, CONV_W - 1, D_FF), lambda b, s, k, pt: (b, 0, 0)),
            pl.BlockSpec((n_tok, ATT_WIDTH), row),
        ],
        scratch_shapes=_finish_scratch(tm) + [
            pltpu.VMEM((CONV_W - 1, D_FF), F32),
            pltpu.VMEM((rows, QK_COLS), BF16),
            pltpu.VMEM((rows, 1), F32),
            pltpu.VMEM((rows, 1), F32),
            pltpu.VMEM((rows, V_DIM), F32),
        ],
    )
    return pl.pallas_call(
        functools.partial(_prompt_finish_kernel, final=final, lam_init=lam_init, n_tok=n_tok,
                          n_sub=n_sub),
        grid_spec=grid_spec,
        out_shape=[
            jax.ShapeDtypeStruct((m, D_MODEL), F32),
            jax.ShapeDtypeStruct((batch, CONV_W - 1, D_FF), F32),
            jax.ShapeDtypeStruct((bs * n_tok, ATT_WIDTH), F32),
        ],
        compiler_params=pltpu.CompilerParams(
            dimension_semantics=("arbitrary", "arbitrary", "arbitrary"),
            vmem_limit_bytes=VMEM_LIMIT),
        name="prompt_finish",
    )(page_table, x2d, att, gm, p2d, *weights, q_s, kn_s, vn_s, *lams, sg,
      *([cache_kt] * npg), *([cache_vr] * npg))


def _sample_finish(layer, x2d, att, gm, p2d, weights, state, n_tok, final):
    m = x2d.shape[0]
    row = lambda i: (0, 0)
    in_specs = [
        pl.BlockSpec((m, D_MODEL), row),
        pl.BlockSpec((m, ATT_WIDTH), row),
        pl.BlockSpec((m, GM_WIDTH), row),
        pl.BlockSpec((m, PLE_DIM), row),
    ] + _finish_weight_specs(layer) + [_const_spec(state.shape)]
    return pl.pallas_call(
        functools.partial(_sample_finish_kernel, final=final, n_tok=n_tok),
        grid=(1,),
        in_specs=in_specs,
        out_specs=[
            pl.BlockSpec((m, D_MODEL), row),
            pl.BlockSpec(state.shape, lambda i: (0, 0, 0)),
        ],
        out_shape=[
            jax.ShapeDtypeStruct((m, D_MODEL), F32),
            jax.ShapeDtypeStruct(state.shape, F32),
        ],
        scratch_shapes=_finish_scratch(m),
        compiler_params=pltpu.CompilerParams(
            dimension_semantics=("arbitrary",), vmem_limit_bytes=VMEM_LIMIT),
        name="sample_finish",
    )(x2d, att, gm, p2d, *weights, state)


def _rope_tables(pos):
    half = ROT_DIM // 2
    inv = ROPE_THETA ** (-jnp.arange(half, dtype=F32) * 2.0 / ROT_DIM)
    ang = pos.astype(F32)[:, None] * inv[None, :]
    cos, sin = jnp.cos(ang), jnp.sin(ang)
    n = pos.shape[0]
    rest = QK_DIM - ROT_DIM
    rc = jnp.concatenate([cos, cos, jnp.ones((n, rest), F32)], axis=1)
    ra = jnp.concatenate([-sin, jnp.zeros((n, half + rest), F32)], axis=1)
    rb = jnp.concatenate([jnp.zeros((n, half), F32), sin, jnp.zeros((n, rest), F32)], axis=1)
    rep = LANES // QK_DIM
    return tuple(jnp.tile(t, (1, rep)) for t in (rc, ra, rb))


def kernel(x_prompt, x_sample, cache_k, cache_v, state_ffn_conv, page_table, p_prompt, p_sample,
           norm_mix, w_in, lambda_q1, lambda_k1, lambda_q2, lambda_k2, subln_g, gm_norm_g,
           gm_w_s, gm_b_s, w_out, norm_ffn, w_ffn_in, conv_w, conv_b, w_ffn_out, norm_ple,
           w_ple_gate, w_ple_proj, norm_final):
    bp, seq, _ = x_prompt.shape
    bs, n_tok, _ = x_sample.shape
    n_phys = cache_k.shape[1]
    n_pages = page_table.shape[1]
    past_len = n_pages * PAGE_SIZE
    mp, ms = bp * seq, bs * n_tok

    w_in_b = w_in.astype(BF16)
    w_out_b = w_out.astype(BF16)
    w1_b = w_ffn_in.astype(BF16)
    w2_b = w_ffn_out.astype(BF16)
    wg_b = w_ple_gate.astype(BF16)
    wp_b = w_ple_proj.astype(BF16)

    rope_p = _rope_tables(jnp.arange(seq))
    rope_s = _rope_tables(past_len + jnp.arange(n_tok))
    rope_s = tuple(jnp.tile(t, (bs, 1)) for t in rope_s)
    gsum = jnp.kron(jnp.eye(GM_GROUPS, dtype=F32),
                    jnp.ones((GM_GROUP_DIM, GM_GROUP_DIM), F32)).astype(BF16)

    wpair = gm_w_s.reshape(DEPTH, GM_GROUPS // 2, 2, CHUNK, CHUNK).transpose(0, 1, 3, 2, 4)
    wpair = wpair.reshape(DEPTH, GM_GROUPS // 2, CHUNK, 2 * CHUNK).astype(BF16)
    bias_p = jnp.repeat(gm_b_s.transpose(0, 2, 1), GM_GROUP_DIM, axis=2)
    t_idx = jnp.arange(n_tok)[:, None]
    d_idx = jnp.arange(n_tok)[None, :]
    src = t_idx - d_idx
    w_small = jnp.tril(gm_w_s[:, :, :n_tok, :n_tok])
    coef = jnp.where(src >= 0, w_small[:, :, t_idx, jnp.maximum(src, 0)], 0.0)
    coef = jnp.repeat(coef.transpose(0, 3, 2, 1), GM_GROUP_DIM, axis=3)
    bias_s = jnp.repeat(gm_b_s[:, :, :n_tok].transpose(0, 2, 1), GM_GROUP_DIM, axis=2)

    cache_kt = cache_k.transpose(0, 1, 3, 4, 5, 2).reshape(DEPTH, n_phys, QK_COLS, PAGE_SIZE)
    cache_vr = cache_v.reshape(DEPTH, n_phys, PAGE_SIZE * ATT_HEADS, V_DIM)

    xp = x_prompt.reshape(mp, D_MODEL)
    xs = x_sample.reshape(ms, D_MODEL)
    pp = p_prompt.reshape(DEPTH, mp, PLE_DIM)
    ps = p_sample.reshape(DEPTH, ms, PLE_DIM)
    kall = jnp.zeros((DEPTH, bp, QK_COLS, seq), F32)
    vall = jnp.zeros((DEPTH, mp * ATT_HEADS, V_DIM), F32)
    nfin = norm_final[None, :]
    cp_l, ks_l, vs_l, cs_l, zs_l = [], [], [], [], []
    for i in range(DEPTH):
        lam_init = 0.8 - 0.6 * math.exp(-0.3 * i)
        final = i == DEPTH - 1
        lams = (lambda_q1[i][None], lambda_k1[i][None], lambda_q2[i][None], lambda_k2[i][None])
        sg = subln_g[i][None]
        gmix = norm_mix[i][None]
        gng = gm_norm_g[i][None]
        weights = (w_out_b, norm_ffn[i][None], w1_b, conv_w[i], conv_b[i][None], w2_b,
                   norm_ple[i][None], wg_b, wp_b, nfin)

        kall, vall, qb, kb, vb, gm_p = _prompt_mix(
            i, xp, gmix, w_in_b, rope_p, gsum, gng, wpair[i], bias_p[i], kall, vall, seq)
        att_p = _prompt_attn(qb, kb, vb, lams, sg, lam_init, bp, seq)
        q_s, k_s, v_s, z_s, gm_s = _sample_mix(
            i, xs, gmix, w_in_b, rope_s, gsum, gng, coef[i], bias_s[i])
        xp, cp, att_s = _prompt_finish(
            i, xp, att_p, gm_p, pp[i], weights, bp, seq, final,
            page_table, q_s, k_s, v_s, lams, sg, cache_kt, cache_vr, lam_init, n_tok)
        xs, cs = _sample_finish(i, xs, att_s, gm_s, ps[i], weights, state_ffn_conv[i], n_tok,
                                final)
        cp_l.append(cp)
        ks_l.append(k_s)
        vs_l.append(v_s)
        zs_l.append(z_s)
        cs_l.append(cs)

    return (
        xp.reshape(bp, seq, D_MODEL),
        xs.reshape(bs, n_tok, D_MODEL),
        kall.reshape(DEPTH, bp, ATT_HEADS, 2, QK_DIM, seq).transpose(0, 1, 5, 2, 3, 4),
        vall.reshape(DEPTH, bp, seq, ATT_HEADS, V_DIM),
        jnp.stack(cp_l),
        jnp.stack(ks_l).reshape(DEPTH, bs, n_tok, ATT_HEADS, 2, QK_DIM),
        jnp.stack(vs_l).reshape(DEPTH, bs, n_tok, ATT_HEADS, V_DIM),
        jnp.stack(cs_l),
        jnp.stack(zs_l).reshape(DEPTH, bs, n_tok, GM_WIDTH),
    )
```

```python
import functools
import math

import jax
import jax.numpy as jnp
from jax import lax
from jax.experimental import pallas as pl
from jax.experimental.pallas import tpu as pltpu

D_MODEL = 1024
DEPTH = 4
PAGE_SIZE = 128
ATT_HEADS = 4
QK_DIM = 64
V_DIM = 2 * QK_DIM
ATT_WIDTH = ATT_HEADS * V_DIM
QK_COLS = ATT_HEADS * 2 * QK_DIM
GM_WIDTH = D_MODEL - ATT_WIDTH
GM_GROUPS = 8
GM_GROUP_DIM = GM_WIDTH // GM_GROUPS
CHUNK = 128
ROT_DIM = QK_DIM // 4
ROPE_THETA = 500000.0
IN_COLS = 2 * QK_COLS + ATT_WIDTH + 2 * GM_WIDTH
D_FF = 2816
CONV_W = 3
PLE_DIM = 256
EPS = 1e-6
NEG = -1e30

LANES = 128
FF_CHUNK = 256
N_FF_CHUNKS = D_FF // FF_CHUNK
VMEM_LIMIT = 56 * 1024 * 1024

TM_MIX = 512
TM_FIN = 512
TQ = 256
PAGES_PER_STEP = 16

F32 = jnp.float32
BF16 = jnp.bfloat16


def _rms(x, g):
    r = lax.rsqrt(jnp.mean(x * x, axis=-1, keepdims=True) + EPS)
    return (x * r) * g


def _sigmoid(x):
    return 1.0 / (1.0 + jnp.exp(-x))


def _dot(a, b):
    return jnp.dot(a, b, preferred_element_type=F32)


def _dot_nt(a, b):
    return lax.dot_general(a, b, (((1,), (1,)), ((), ())), preferred_element_type=F32)


def _const_spec(shape, single=True):
    nd = len(shape)
    idx = lambda *_: (0,) * nd
    if single:
        return pl.BlockSpec(shape, idx, pipeline_mode=pl.Buffered(1))
    return pl.BlockSpec(shape, idx)


def _layer_spec(shape, layer):
    nd = len(shape)
    return pl.BlockSpec((None,) + tuple(shape), lambda *_: (layer,) + (0,) * nd,
                        pipeline_mode=pl.Buffered(1))


def _project(x_ref, g_ref, w_ref, rc_ref, ra_ref, rb_ref, gsum_ref, gng_ref):
    h = _rms(x_ref[...], g_ref[...]).astype(BF16)
    z = _dot(h, w_ref[...])
    rc, ra, rb = rc_ref[...], ra_ref[...], rb_ref[...]

    def rope(t):
        cols = []
        for j in range(QK_COLS // LANES):
            tj = t[:, j * LANES:(j + 1) * LANES]
            up = pltpu.roll(tj, LANES - ROT_DIM // 2, 1)
            dn = pltpu.roll(tj, ROT_DIM // 2, 1)
            cols.append(tj * rc + up * ra + dn * rb)
        return jnp.concatenate(cols, axis=1)

    q = rope(z[:, :QK_COLS])
    k = rope(z[:, QK_COLS:2 * QK_COLS])
    o = 2 * QK_COLS
    v = z[:, o:o + ATT_WIDTH]
    o += ATT_WIDTH
    u = z[:, o:o + GM_WIDTH]
    vg = z[:, o + GM_WIDTH:]
    ss = _dot((vg * vg).astype(BF16), gsum_ref[...]) * (1.0 / GM_GROUP_DIM)
    zg = (vg * lax.rsqrt(ss + EPS)) * gng_ref[...]
    return q, k, v, u, zg


def _prompt_mix_kernel(x_ref, g_ref, w_ref, rc_ref, ra_ref, rb_ref, gsum_ref, gng_ref,
                       wpair_ref, bias_ref, kall_in, vall_in,
                       kall_ref, vall_ref, qb_ref, ktb_ref, vb_ref, gm_ref):
    del kall_in, vall_in
    tm = x_ref.shape[0]
    q, k, v, u, zg = _project(x_ref, g_ref, w_ref, rc_ref, ra_ref, rb_ref, gsum_ref, gng_ref)
    kt = k.T
    kall_ref[...] = kt
    ktb_ref[...] = kt.astype(BF16)
    for h in range(ATT_HEADS):
        vall_ref[pl.ds(h, tm, stride=ATT_HEADS), :] = v[:, h * V_DIM:(h + 1) * V_DIM]
    qb_ref[...] = (q * (QK_DIM ** -0.5)).astype(BF16)
    vb_ref[...] = v.astype(BF16)

    zb = zg.astype(BF16)
    lane = lax.broadcasted_iota(jnp.int32, (CHUNK, LANES), 1)
    wrow = lax.broadcasted_iota(jnp.int32, (CHUNK, 2 * CHUNK), 0)
    wcol = lax.broadcasted_iota(jnp.int32, (CHUNK, 2 * CHUNK), 1)
    tril = (wcol % CHUNK) <= wrow
    bias = bias_ref[...]
    zero = jnp.zeros((CHUNK, LANES), BF16)
    for pr in range(GM_WIDTH // LANES):
        wp = jnp.where(tril, wpair_ref[pr], jnp.zeros((), BF16))
        for c in range(tm // CHUNK):
            rows = slice(c * CHUNK, (c + 1) * CHUNK)
            cols = slice(pr * LANES, (pr + 1) * LANES)
            zp = zb[rows, cols]
            rhs = jnp.concatenate([jnp.where(lane < GM_GROUP_DIM, zp, zero),
                                   jnp.where(lane >= GM_GROUP_DIM, zp, zero)], axis=0)
            mix = _dot(wp, rhs) + bias[:, cols]
            gm_ref[rows, cols] = (u[rows, cols] * mix).astype(BF16)


def _sample_mix_kernel(x_ref, g_ref, w_ref, rc_ref, ra_ref, rb_ref, gsum_ref, gng_ref,
                       coef_ref, bias_ref,
                       q_ref, k_ref, v_ref, zs_ref, gm_ref):
    q, k, v, u, zg = _project(x_ref, g_ref, w_ref, rc_ref, ra_ref, rb_ref, gsum_ref, gng_ref)
    q_ref[...] = q * (QK_DIM ** -0.5)
    k_ref[...] = k
    v_ref[...] = v
    zs_ref[...] = zg
    m, w = zg.shape
    nq = coef_ref.shape[0]
    z3 = zg.reshape(m // nq, nq, w)
    mix = coef_ref[0][None] * z3
    for d in range(1, nq):
        mix = mix + coef_ref[d][None] * pltpu.roll(z3, d, 1)
    mix = mix + bias_ref[...][None]
    gm_ref[...] = (u * mix.reshape(m, w)).astype(BF16)


def _mix_common_specs(layer, tm, n_seq_tiles):
    row = lambda r: (r, 0)
    pos = lambda r: (r % n_seq_tiles, 0)
    return [
        pl.BlockSpec((tm, D_MODEL), row),
        _const_spec((1, D_MODEL)),
        _layer_spec((D_MODEL, IN_COLS), layer),
        pl.BlockSpec((tm, LANES), pos),
        pl.BlockSpec((tm, LANES), pos),
        pl.BlockSpec((tm, LANES), pos),
        _const_spec((GM_WIDTH, GM_WIDTH)),
        _const_spec((1, GM_WIDTH)),
    ]


def _prompt_mix(layer, x2d, g, w_in_b, rope, gsum, gng, wpair, bias, kall, vall, seq):
    m = x2d.shape[0]
    tm = TM_MIX
    n_seq_tiles = seq // tm
    row = lambda r: (r, 0)
    in_specs = _mix_common_specs(layer, tm, n_seq_tiles) + [
        _const_spec(wpair.shape),
        _const_spec(bias.shape),
        pl.BlockSpec(memory_space=pl.ANY),
        pl.BlockSpec(memory_space=pl.ANY),
    ]
    out_specs = [
        pl.BlockSpec((None, None, QK_COLS, tm),
                     lambda r: (layer, r // n_seq_tiles, 0, r % n_seq_tiles)),
        pl.BlockSpec((None, tm * ATT_HEADS, V_DIM), lambda r: (layer, r, 0)),
        pl.BlockSpec((tm, QK_COLS), row),
        pl.BlockSpec((None, QK_COLS, tm), lambda r: (r // n_seq_tiles, 0, r % n_seq_tiles)),
        pl.BlockSpec((tm, ATT_WIDTH), row),
        pl.BlockSpec((tm, GM_WIDTH), row),
    ]
    out_shape = [
        jax.ShapeDtypeStruct(kall.shape, F32),
        jax.ShapeDtypeStruct(vall.shape, F32),
        jax.ShapeDtypeStruct((m, QK_COLS), BF16),
        jax.ShapeDtypeStruct((m // seq, QK_COLS, seq), BF16),
        jax.ShapeDtypeStruct((m, ATT_WIDTH), BF16),
        jax.ShapeDtypeStruct((m, GM_WIDTH), BF16),
    ]
    return pl.pallas_call(
        _prompt_mix_kernel,
        grid=(m // tm,),
        in_specs=in_specs,
        out_specs=out_specs,
        out_shape=out_shape,
        input_output_aliases={10: 0, 11: 1},
        compiler_params=pltpu.CompilerParams(
            dimension_semantics=("arbitrary",), vmem_limit_bytes=VMEM_LIMIT),
        name="prompt_mix",
    )(x2d, g, w_in_b, *rope, gsum, gng, wpair, bias, kall, vall)


def _sample_mix(layer, x2d, g, w_in_b, rope, gsum, gng, coef, bias):
    m = x2d.shape[0]
    row = lambda r: (r, 0)
    in_specs = _mix_common_specs(layer, m, 1) + [
        _const_spec(coef.shape),
        _const_spec(bias.shape),
    ]
    out_specs = [pl.BlockSpec((m, QK_COLS), row)] * 5
    out_shape = [jax.ShapeDtypeStruct((m, QK_COLS), F32)] * 4 + [
        jax.ShapeDtypeStruct((m, GM_WIDTH), BF16)]
    return pl.pallas_call(
        _sample_mix_kernel,
        grid=(1,),
        in_specs=in_specs,
        out_specs=out_specs,
        out_shape=out_shape,
        compiler_params=pltpu.CompilerParams(
            dimension_semantics=("arbitrary",), vmem_limit_bytes=VMEM_LIMIT),
        name="sample_mix",
    )(x2d, g, w_in_b, *rope, gsum, gng, coef, bias)


def _diff_lambda(lq1, lk1, lq2, lk2, lam_init):
    a = jnp.exp(jnp.sum(lq1[...] * lk1[...], axis=-1, keepdims=True))
    b = jnp.exp(jnp.sum(lq2[...] * lk2[...], axis=-1, keepdims=True))
    return a - b + lam_init


def _prompt_attn_kernel(q_ref, kt_ref, v_ref, lq1, lk1, lq2, lk2, sg_ref, o_ref, *, lam_init):
    seq = q_ref.shape[0]
    tq = TQ
    lam = _diff_lambda(lq1, lk1, lq2, lk2, lam_init)
    sg = sg_ref[...]
    lane = lax.broadcasted_iota(jnp.int32, (tq, LANES), 1)
    r = lax.broadcasted_iota(jnp.int32, (tq, tq), 0)
    c = lax.broadcasted_iota(jnp.int32, (tq, tq), 1)
    keep = jnp.concatenate([c <= r, c <= r], axis=0)
    zero = jnp.zeros((tq, LANES), BF16)
    for qi in range(seq // tq):
        n = qi * tq
        q = q_ref[n:n + tq, :]
        qs = jnp.concatenate([jnp.where(lane < QK_DIM, q, zero),
                              jnp.where(lane >= QK_DIM, q, zero)], axis=0)
        s_d = jnp.where(keep, _dot(qs, kt_ref[:, n:n + tq]), NEG)
        m = jnp.max(s_d, axis=-1, keepdims=True)
        if qi:
            s_f = _dot(qs, kt_ref[:, :n])
            m = jnp.maximum(m, jnp.max(s_f, axis=-1, keepdims=True))
        p_d = jnp.exp(s_d - m)
        l = jnp.sum(p_d, axis=-1, keepdims=True)
        o = _dot(p_d.astype(BF16), v_ref[n:n + tq, :])
        if qi:
            p_f = jnp.exp(s_f - m)
            l = l + jnp.sum(p_f, axis=-1, keepdims=True)
            o = o + _dot(p_f.astype(BF16), v_ref[:n, :])
        o = o / l
        att = o[:tq] - lam * o[tq:]
        o_ref[n:n + tq, :] = (_rms(att, sg) * (1.0 - lam_init)).astype(BF16)


def _prompt_attn(qb, ktb, vb, lams, sg, lam_init, batch, seq):
    m = qb.shape[0]
    small = [_const_spec((1, QK_DIM), single=False)] * 4 + [_const_spec((1, V_DIM), single=False)]
    return pl.pallas_call(
        functools.partial(_prompt_attn_kernel, lam_init=lam_init),
        grid=(batch, ATT_HEADS),
        in_specs=[
            pl.BlockSpec((seq, LANES), lambda b, h: (b, h)),
            pl.BlockSpec((None, LANES, seq), lambda b, h: (b, h, 0)),
            pl.BlockSpec((seq, LANES), lambda b, h: (b, h)),
        ] + small,
        out_specs=pl.BlockSpec((seq, LANES), lambda b, h: (b, h)),
        out_shape=jax.ShapeDtypeStruct((m, ATT_WIDTH), BF16),
        compiler_params=pltpu.CompilerParams(
            dimension_semantics=("arbitrary", "arbitrary"), vmem_limit_bytes=VMEM_LIMIT),
        name="prompt_attn",
    )(qb, ktb, vb, *lams, sg)


def _sample_attn_step(first, last, q_ref, kn_ref, vn_ref, lam_refs, sg_ref, k_refs, v_refs,
                      o_ref, qbd_ref, m_ref, l_ref, acc_ref, lam_init, n_tok):
    rows = qbd_ref.shape[0]
    hrows = rows // ATT_HEADS

    if first:
        qt = jnp.concatenate([q_ref[...]] * (rows // n_tok), axis=0)
        r = lax.broadcasted_iota(jnp.int32, qt.shape, 0)
        c = lax.broadcasted_iota(jnp.int32, qt.shape, 1)
        qbd = jnp.where(r // n_tok == c // QK_DIM, qt, 0.0)
        qbd_ref[...] = qbd.astype(BF16)
        s = _dot_nt(qbd, kn_ref[...])
        rr = lax.broadcasted_iota(jnp.int32, s.shape, 0)
        cc = lax.broadcasted_iota(jnp.int32, s.shape, 1)
        s = jnp.where(cc <= rr % n_tok, s, NEG)
        m0 = jnp.max(s, axis=-1, keepdims=True)
        p = jnp.exp(s - m0)
        m_ref[...] = m0
        l_ref[...] = jnp.sum(p, axis=-1, keepdims=True)
        vn = vn_ref[...]
        acc_ref[...] = jnp.concatenate(
            [_dot(p[h * hrows:(h + 1) * hrows], vn[:, h * V_DIM:(h + 1) * V_DIM])
             for h in range(ATT_HEADS)], axis=0)

    qbd = qbd_ref[...]
    s = jnp.concatenate([_dot(qbd, kr[...].astype(BF16)) for kr in k_refs], axis=1)
    m_old = m_ref[...]
    m_new = jnp.maximum(m_old, jnp.max(s, axis=-1, keepdims=True))
    alpha = jnp.exp(m_old - m_new)
    p = jnp.exp(s - m_new)
    l_ref[...] = alpha * l_ref[...] + jnp.sum(p, axis=-1, keepdims=True)
    pb = p.astype(BF16)
    parts = []
    for h in range(ATT_HEADS):
        vh = jnp.concatenate(
            [vr[pl.ds(h, PAGE_SIZE, stride=ATT_HEADS), :].astype(BF16) for vr in v_refs], axis=0)
        parts.append(_dot(pb[h * hrows:(h + 1) * hrows], vh))
    acc_ref[...] = alpha * acc_ref[...] + jnp.concatenate(parts, axis=0)
    m_ref[...] = m_new

    if last:
        o = acc_ref[...] / l_ref[...]
        lam = _diff_lambda(*lam_refs, lam_init)
        sg = sg_ref[...]
        for h in range(ATT_HEADS):
            r0 = h * hrows
            att = o[r0:r0 + n_tok] - lam * o[r0 + n_tok:r0 + 2 * n_tok]
            o_ref[:, h * V_DIM:(h + 1) * V_DIM] = _rms(att, sg) * (1.0 - lam_init)


def _ffn_pre(x_ref, att_ref, gm_ref, wo_ref, nf_ref, x1_ref, hn_ref, acc_ref):
    mix = jnp.concatenate([att_ref[...].astype(BF16), gm_ref[...]], axis=1)
    x1 = x_ref[...] + _dot(mix, wo_ref[...])
    x1_ref[...] = x1
    hn_ref[...] = _rms(x1, nf_ref[...]).astype(BF16)
    acc_ref[...] = jnp.zeros(acc_ref.shape, F32)


def _ffn_chunk(c, hn_ref, w1_ref, cw_ref, cb_ref, w2_ref, acc_ref, conv_fn):
    cols = slice(c * FF_CHUNK, (c + 1) * FF_CHUNK)
    gcols = slice(D_FF + c * FF_CHUNK, D_FF + (c + 1) * FF_CHUNK)
    hn = hn_ref[...]
    a = _dot(hn, w1_ref[:, cols])
    gate_in = _dot(hn, w1_ref[:, gcols])
    cw = cw_ref[:, cols]
    a1, a2 = conv_fn(cols, a)
    ac = cb_ref[:, cols] + cw[0:1] * a2 + cw[1:2] * a1 + cw[2:3] * a
    hid = (ac * _sigmoid(ac)) * gate_in
    acc_ref[...] += _dot(hid.astype(BF16), w2_ref[cols, :])


def _ffn_post(x1_ref, acc_ref, p_ref, np_ref, wg_ref, wp_ref, nfin_ref, y_ref, final):
    x2 = x1_ref[...] + acc_ref[...]
    gate = _sigmoid(_dot(_rms(x2, np_ref[...]).astype(BF16), wg_ref[...]))
    x3 = x2 + gate * _dot(p_ref[...].astype(BF16), wp_ref[...])
    y_ref[...] = _rms(x3, nfin_ref[...]) if final else x3


def _split_chunks(n_parts):
    base, extra = divmod(N_FF_CHUNKS, n_parts)
    out, start = [], 0
    for i in range(n_parts):
        size = base + (1 if i < extra else 0)
        out.append(range(start, start + size))
        start += size
    return out


def _prompt_finish_kernel(pt_ref, x_ref, att_ref, gm_ref, p_ref, wo_ref, nf_ref, w1_ref, cw_ref,
                          cb_ref, w2_ref, np_ref, wg_ref, wp_ref, nfin_ref,
                          q_ref, kn_ref, vn_ref, lq1, lk1, lq2, lk2, sg_ref, ck_hbm, cv_hbm,
                          y_ref, cs_ref, atts_ref,
                          hn_ref, acc_ref, carry_ref, qbd_ref, m_ref, l_ref, accs_ref,
                          kbuf, vbuf, sem, *, layer, final, lam_init, n_tok, n_sub):
    npg = PAGES_PER_STEP
    x1_ref = y_ref
    s = pl.program_id(1)
    k = pl.program_id(2)
    tm = x_ref.shape[0]
    tile = pl.program_id(0) * pl.num_programs(1) + s
    n_tiles = pl.num_programs(0) * pl.num_programs(1)

    def page_copy(is_key, slot, i, page):
        src, dst = (ck_hbm, kbuf) if is_key else (cv_hbm, vbuf)
        return pltpu.make_async_copy(src.at[layer, page], dst.at[slot, i],
                                     sem.at[slot, 0 if is_key else 1])

    def start_pages(slot, row, sub):
        for i in range(npg):
            page = pt_ref[row, sub * npg + i]
            page_copy(True, slot, i, page).start()
            page_copy(False, slot, i, page).start()

    def wait_pages(slot):
        for i in range(npg):
            page_copy(True, slot, i, 0).wait()
            page_copy(False, slot, i, 0).wait()

    def conv_fn(cols, a):
        prev = carry_ref[:, cols]
        row = lax.broadcasted_iota(jnp.int32, a.shape, 0)
        a1 = jnp.where(row == 0, prev[1:2], pltpu.roll(a, 1, 0))
        a2 = jnp.where(row == 0, prev[0:1], jnp.where(row == 1, prev[1:2], pltpu.roll(a, 2, 0)))
        tail = a[tm - (CONV_W - 1):, :]
        carry_ref[:, cols] = tail
        cs_ref[:, cols] = tail
        return a1, a2

    @pl.when(jnp.logical_and(s == 0, k == 0))
    def _():
        carry_ref[...] = jnp.zeros(carry_ref.shape, F32)

    @pl.when(jnp.logical_and(tile == 0, k == 0))
    def _():
        start_pages(0, 0, 0)

    for sub, chunks in enumerate(_split_chunks(n_sub)):
        @pl.when(k == sub)
        def _(sub=sub, chunks=chunks):
            slot = sub % 2
            wait_pages(slot)
            if sub < n_sub - 1:
                start_pages(1 - slot, tile, sub + 1)
            else:
                @pl.when(tile + 1 < n_tiles)
                def _():
                    start_pages(1 - slot, tile + 1, 0)
            k_refs = [kbuf.at[slot, i] for i in range(npg)]
            v_refs = [vbuf.at[slot, i] for i in range(npg)]
            _sample_attn_step(sub == 0, sub == n_sub - 1, q_ref, kn_ref, vn_ref,
                              (lq1, lk1, lq2, lk2), sg_ref, k_refs, v_refs, atts_ref,
                              qbd_ref, m_ref, l_ref, accs_ref, lam_init, n_tok)
            if sub == 0:
                _ffn_pre(x_ref, att_ref, gm_ref, wo_ref, nf_ref, x1_ref, hn_ref, acc_ref)
            for c in chunks:
                _ffn_chunk(c, hn_ref, w1_ref, cw_ref, cb_ref, w2_ref, acc_ref, conv_fn)
            if sub == n_sub - 1:
                _ffn_post(x1_ref, acc_ref, p_ref, np_ref, wg_ref, wp_ref, nfin_ref, y_ref, final)


def _sample_finish_kernel(x_ref, att_ref, gm_ref, p_ref, wo_ref, nf_ref, w1_ref, cw_ref,
                          cb_ref, w2_ref, np_ref, wg_ref, wp_ref, nfin_ref, st_ref,
                          y_ref, cs_ref, hn_ref, acc_ref, *, final, n_tok):
    x1_ref = y_ref

    def conv_fn(cols, a):
        m, w = a.shape
        a3 = a.reshape(m // n_tok, n_tok, w)
        prev = st_ref[:, :, cols]
        t = lax.broadcasted_iota(jnp.int32, a3.shape, 1)
        p0, p1 = prev[:, 0:1, :], prev[:, 1:2, :]
        a1 = jnp.where(t == 0, p1, pltpu.roll(a3, 1, 1))
        a2 = jnp.where(t == 0, p0, jnp.where(t == 1, p1, pltpu.roll(a3, 2, 1)))
        cs_ref[:, :, cols] = a3[:, n_tok - (CONV_W - 1):, :]
        return a1.reshape(m, w), a2.reshape(m, w)

    _ffn_pre(x_ref, att_ref, gm_ref, wo_ref, nf_ref, x1_ref, hn_ref, acc_ref)
    for c in range(N_FF_CHUNKS):
        _ffn_chunk(c, hn_ref, w1_ref, cw_ref, cb_ref, w2_ref, acc_ref, conv_fn)
    _ffn_post(x1_ref, acc_ref, p_ref, np_ref, wg_ref, wp_ref, nfin_ref, y_ref, final)


def _finish_weight_specs(layer):
    return [
        _layer_spec((D_MODEL, D_MODEL), layer),
        _const_spec((1, D_MODEL)),
        _layer_spec((D_MODEL, 2 * D_FF), layer),
        _const_spec((CONV_W, D_FF)),
        _const_spec((1, D_FF)),
        _layer_spec((D_FF, D_MODEL), layer),
        _const_spec((1, D_MODEL)),
        _layer_spec((D_MODEL, D_MODEL), layer),
        _layer_spec((PLE_DIM, D_MODEL), layer),
        _const_spec((1, D_MODEL)),
    ]


def _finish_scratch(tm):
    return [
        pltpu.VMEM((tm, D_MODEL), BF16),
        pltpu.VMEM((tm, D_MODEL), F32),
    ]


def _prompt_finish(layer, x2d, att, gm, p2d, weights, batch, seq, final,
                   page_table, q_s, kn_s, vn_s, lams, sg, cache_kt, cache_vr, lam_init, n_tok):
    m = x2d.shape[0]
    tm = TM_FIN
    ns = seq // tm
    npg = PAGES_PER_STEP
    bs, n_pages = page_table.shape
    n_sub = n_pages // npg
    assert bs == batch * ns and n_pages == n_sub * npg
    assert n_sub <= N_FF_CHUNKS and n_sub % 2 == 0
    rows = ATT_HEADS * 2 * n_tok
    row = lambda b, s, k, pt: (b * ns + s, 0)
    fixed = lambda *_: (0, 0)
    in_specs = [
        pl.BlockSpec((tm, D_MODEL), row),
        pl.BlockSpec((tm, ATT_WIDTH), row),
        pl.BlockSpec((tm, GM_WIDTH), row),
        pl.BlockSpec((tm, PLE_DIM), row),
    ] + _finish_weight_specs(layer) + [
        pl.BlockSpec((n_tok, QK_COLS), row),
        pl.BlockSpec((n_tok, QK_COLS), row),
        pl.BlockSpec((n_tok, ATT_WIDTH), row),
    ] + [pl.BlockSpec((1, QK_DIM), fixed)] * 4 + [pl.BlockSpec((1, V_DIM), fixed)] + [
        pl.BlockSpec(memory_space=pl.ANY),
        pl.BlockSpec(memory_space=pl.ANY),
    ]
    page_buf = pltpu.VMEM((2, npg) + cache_kt.shape[2:], F32)
    grid_spec = pltpu.PrefetchScalarGridSpec(
        num_scalar_prefetch=1,
        grid=(batch, ns, n_sub),
        in_specs=in_specs,
        out_specs=[
            pl.BlockSpec((tm, D_MODEL), row),
            pl.BlockSpec((None, CONV_W - 1, D_FF), lambda b, s, k, pt: (b, 0, 0)),
            pl.BlockSpec((n_tok, ATT_WIDTH), row),
        ],
        scratch_shapes=_finish_scratch(tm) + [
            pltpu.VMEM((CONV_W - 1, D_FF), F32),
            pltpu.VMEM((rows, QK_COLS), BF16),
            pltpu.VMEM((rows, 1), F32),
            pltpu.VMEM((rows, 1), F32),
            pltpu.VMEM((rows, V_DIM), F32),
            page_buf,
            page_buf,
            pltpu.SemaphoreType.DMA((2, 2)),
        ],
    )
    return pl.pallas_call(
        functools.partial(_prompt_finish_kernel, layer=layer, final=final, lam_init=lam_init,
                          n_tok=n_tok, n_sub=n_sub),
        grid_spec=grid_spec,
        out_shape=[
            jax.ShapeDtypeStruct((m, D_MODEL), F32),
            jax.ShapeDtypeStruct((batch, CONV_W - 1, D_FF), F32),
            jax.ShapeDtypeStruct((bs * n_tok, ATT_WIDTH), F32),
        ],
        compiler_params=pltpu.CompilerParams(
            dimension_semantics=("arbitrary", "arbitrary", "arbitrary"),
            vmem_limit_bytes=VMEM_LIMIT),
        name="prompt_finish",
    )(page_table, x2d, att, gm, p2d, *weights, q_s, kn_s, vn_s, *lams, sg, cache_kt, cache_vr)


def _sample_finish(layer, x2d, att, gm, p2d, weights, state, n_tok, final):
    m = x2d.shape[0]
    row = lambda i: (0, 0)
    in_specs = [
        pl.BlockSpec((m, D_MODEL), row),
        pl.BlockSpec((m, ATT_WIDTH), row),
        pl.BlockSpec((m, GM_WIDTH), row),
        pl.BlockSpec((m, PLE_DIM), row),
    ] + _finish_weight_specs(layer) + [_const_spec(state.shape)]
    return pl.pallas_call(
        functools.partial(_sample_finish_kernel, final=final, n_tok=n_tok),
        grid=(1,),
        in_specs=in_specs,
        out_specs=[
            pl.BlockSpec((m, D_MODEL), row),
            pl.BlockSpec(state.shape, lambda i: (0, 0, 0)),
        ],
        out_shape=[
            jax.ShapeDtypeStruct((m, D_MODEL), F32),
            jax.ShapeDtypeStruct(state.shape, F32),
        ],
        scratch_shapes=_finish_scratch(m),
        compiler_params=pltpu.CompilerParams(
            dimension_semantics=("arbitrary",), vmem_limit_bytes=VMEM_LIMIT),
        name="sample_finish",
    )(x2d, att, gm, p2d, *weights, state)


def _rope_tables(pos):
    half = ROT_DIM // 2
    inv = ROPE_THETA ** (-jnp.arange(half, dtype=F32) * 2.0 / ROT_DIM)
    ang = pos.astype(F32)[:, None] * inv[None, :]
    cos, sin = jnp.cos(ang), jnp.sin(ang)
    n = pos.shape[0]
    rest = QK_DIM - ROT_DIM
    rc = jnp.concatenate([cos, cos, jnp.ones((n, rest), F32)], axis=1)
    ra = jnp.concatenate([-sin, jnp.zeros((n, half + rest), F32)], axis=1)
    rb = jnp.concatenate([jnp.zeros((n, half), F32), sin, jnp.zeros((n, rest), F32)], axis=1)
    rep = LANES // QK_DIM
    return tuple(jnp.tile(t, (1, rep)) for t in (rc, ra, rb))


def kernel(x_prompt, x_sample, cache_k, cache_v, state_ffn_conv, page_table, p_prompt, p_sample,
           norm_mix, w_in, lambda_q1, lambda_k1, lambda_q2, lambda_k2, subln_g, gm_norm_g,
           gm_w_s, gm_b_s, w_out, norm_ffn, w_ffn_in, conv_w, conv_b, w_ffn_out, norm_ple,
           w_ple_gate, w_ple_proj, norm_final):
    bp, seq, _ = x_prompt.shape
    bs, n_tok, _ = x_sample.shape
    n_phys = cache_k.shape[1]
    n_pages = page_table.shape[1]
    past_len = n_pages * PAGE_SIZE
    mp, ms = bp * seq, bs * n_tok

    w_in_b = w_in.astype(BF16)
    w_out_b = w_out.astype(BF16)
    w1_b = w_ffn_in.astype(BF16)
    w2_b = w_ffn_out.astype(BF16)
    wg_b = w_ple_gate.astype(BF16)
    wp_b = w_ple_proj.astype(BF16)

    rope_p = _rope_tables(jnp.arange(seq))
    rope_s = _rope_tables(past_len + jnp.arange(n_tok))
    rope_s = tuple(jnp.tile(t, (bs, 1)) for t in rope_s)
    gsum = jnp.kron(jnp.eye(GM_GROUPS, dtype=F32),
                    jnp.ones((GM_GROUP_DIM, GM_GROUP_DIM), F32)).astype(BF16)

    wpair = gm_w_s.reshape(DEPTH, GM_GROUPS // 2, 2, CHUNK, CHUNK).transpose(0, 1, 3, 2, 4)
    wpair = wpair.reshape(DEPTH, GM_GROUPS // 2, CHUNK, 2 * CHUNK).astype(BF16)
    bias_p = jnp.repeat(gm_b_s.transpose(0, 2, 1), GM_GROUP_DIM, axis=2)
    t_idx = jnp.arange(n_tok)[:, None]
    d_idx = jnp.arange(n_tok)[None, :]
    src = t_idx - d_idx
    w_small = jnp.tril(gm_w_s[:, :, :n_tok, :n_tok])
    coef = jnp.where(src >= 0, w_small[:, :, t_idx, jnp.maximum(src, 0)], 0.0)
    coef = jnp.repeat(coef.transpose(0, 3, 2, 1), GM_GROUP_DIM, axis=3)
    bias_s = jnp.repeat(gm_b_s[:, :, :n_tok].transpose(0, 2, 1), GM_GROUP_DIM, axis=2)

    cache_kt = cache_k.transpose(0, 1, 3, 4, 5, 2).reshape(DEPTH, n_phys, QK_COLS, PAGE_SIZE)
    cache_vr = cache_v.reshape(DEPTH, n_phys, PAGE_SIZE * ATT_HEADS, V_DIM)

    xp = x_prompt.reshape(mp, D_MODEL)
    xs = x_sample.reshape(ms, D_MODEL)
    pp = p_prompt.reshape(DEPTH, mp, PLE_DIM)
    ps = p_sample.reshape(DEPTH, ms, PLE_DIM)
    kall = jnp.zeros((DEPTH, bp, QK_COLS, seq), F32)
    vall = jnp.zeros((DEPTH, mp * ATT_HEADS, V_DIM), F32)
    nfin = norm_final[None, :]
    cp_l, ks_l, vs_l, cs_l, zs_l = [], [], [], [], []
    for i in range(DEPTH):
        lam_init = 0.8 - 0.6 * math.exp(-0.3 * i)
        final = i == DEPTH - 1
        lams = (lambda_q1[i][None], lambda_k1[i][None], lambda_q2[i][None], lambda_k2[i][None])
        sg = subln_g[i][None]
        gmix = norm_mix[i][None]
        gng = gm_norm_g[i][None]
        weights = (w_out_b, norm_ffn[i][None], w1_b, conv_w[i], conv_b[i][None], w2_b,
                   norm_ple[i][None], wg_b, wp_b, nfin)

        kall, vall, qb, kb, vb, gm_p = _prompt_mix(
            i, xp, gmix, w_in_b, rope_p, gsum, gng, wpair[i], bias_p[i], kall, vall, seq)
        att_p = _prompt_attn(qb, kb, vb, lams, sg, lam_init, bp, seq)
        q_s, k_s, v_s, z_s, gm_s = _sample_mix(
            i, xs, gmix, w_in_b, rope_s, gsum, gng, coef[i], bias_s[i])
        xp, cp, att_s = _prompt_finish(
            i, xp, att_p, gm_p, pp[i], weights, bp, seq, final,
            page_table, q_s, k_s, v_s, lams, sg, cache_kt, cache_vr, lam_init, n_tok)
        xs, cs = _sample_finish(i, xs, att_s, gm_s, ps[i], weights, state_ffn_conv[i], n_tok,
                                final)
        cp_l.append(cp)
        ks_l.append(k_s)
        vs_l.append(v_s)
        zs_l.append(z_s)
        cs_l.append(cs)

    return (
        xp.reshape(bp, seq, D_MODEL),
        xs.reshape(bs, n_tok, D_MODEL),
        kall.reshape(DEPTH, bp, ATT_HEADS, 2, QK_DIM, seq).transpose(0, 1, 5, 2, 3, 4),
        vall.reshape(DEPTH, bp, seq, ATT_HEADS, V_DIM),
        jnp.stack(cp_l),
        jnp.stack(ks_l).reshape(DEPTH, bs, n_tok, ATT_HEADS, 2, QK_DIM),
        jnp.stack(vs_l).reshape(DEPTH, bs, n_tok, ATT_HEADS, V_DIM),
        jnp.stack(cs_l),
        jnp.stack(zs_l).reshape(DEPTH, bs, n_tok, GM_WIDTH),
    )
```

```python
import functools
import math

import jax
import jax.numpy as jnp
from jax import lax
from jax.experimental import pallas as pl
from jax.experimental.pallas import tpu as pltpu

D_MODEL = 1024
DEPTH = 4
PAGE_SIZE = 128
ATT_HEADS = 4
QK_DIM = 64
V_DIM = 2 * QK_DIM
ATT_WIDTH = ATT_HEADS * V_DIM
QK_COLS = ATT_HEADS * 2 * QK_DIM
GM_WIDTH = D_MODEL - ATT_WIDTH
GM_GROUPS = 8
GM_GROUP_DIM = GM_WIDTH // GM_GROUPS
CHUNK = 128
ROT_DIM = QK_DIM // 4
ROPE_THETA = 500000.0
IN_COLS = 2 * QK_COLS + ATT_WIDTH + 2 * GM_WIDTH
D_FF = 2816
CONV_W = 3
PLE_DIM = 256
EPS = 1e-6
NEG = -1e30
LOG2_E = math.log2(math.e)

LANES = 128
FF_CHUNK = 256
N_FF_CHUNKS = D_FF // FF_CHUNK
VMEM_LIMIT = 56 * 1024 * 1024

SAMPLE_FFN_SLABS = 4
TM_MIX = 1024
TM_FIN = 512
TQ = 256
PAGES_PER_STEP = 16

F32 = jnp.float32
BF16 = jnp.bfloat16


def _rms(x, g):
    r = lax.rsqrt(jnp.mean(x * x, axis=-1, keepdims=True) + EPS)
    return (x * r) * g


def _sigmoid(x):
    return 1.0 / (1.0 + jnp.exp(-x))


def _dot(a, b):
    return jnp.dot(a, b, preferred_element_type=F32)


def _dot_nt(a, b):
    return lax.dot_general(a, b, (((1,), (1,)), ((), ())), preferred_element_type=F32)


def _const_spec(shape, single=True):
    nd = len(shape)
    idx = lambda *_: (0,) * nd
    if single:
        return pl.BlockSpec(shape, idx, pipeline_mode=pl.Buffered(1))
    return pl.BlockSpec(shape, idx)


def _layer_spec(shape, layer):
    nd = len(shape)
    return pl.BlockSpec((None,) + tuple(shape), lambda *_: (layer,) + (0,) * nd,
                        pipeline_mode=pl.Buffered(1))


def _project(x_ref, g_ref, w_ref, rc_ref, ra_ref, rb_ref, gsum_ref, gng_ref):
    h = _rms(x_ref[...], g_ref[...]).astype(BF16)
    z = _dot(h, w_ref[...])
    rc, ra, rb = rc_ref[...], ra_ref[...], rb_ref[...]

    def rope(t):
        cols = []
        for j in range(QK_COLS // LANES):
            tj = t[:, j * LANES:(j + 1) * LANES]
            up = pltpu.roll(tj, LANES - ROT_DIM // 2, 1)
            dn = pltpu.roll(tj, ROT_DIM // 2, 1)
            cols.append(tj * rc + up * ra + dn * rb)
        return jnp.concatenate(cols, axis=1)

    q = rope(z[:, :QK_COLS])
    k = rope(z[:, QK_COLS:2 * QK_COLS])
    o = 2 * QK_COLS
    v = z[:, o:o + ATT_WIDTH]
    o += ATT_WIDTH
    u = z[:, o:o + GM_WIDTH]
    vg = z[:, o + GM_WIDTH:]
    ss = _dot((vg * vg).astype(BF16), gsum_ref[...]) * (1.0 / GM_GROUP_DIM)
    zg = (vg * lax.rsqrt(ss + EPS)) * gng_ref[...]
    return q, k, v, u, zg


def _prompt_mix_kernel(x_ref, g_ref, w_ref, rc_ref, ra_ref, rb_ref, gsum_ref, gng_ref,
                       wpair_ref, bias_ref, kall_in, vall_in,
                       kall_ref, vall_ref, qb_ref, ktb_ref, vb_ref, gm_ref):
    del kall_in, vall_in
    tm = x_ref.shape[0]
    q, k, v, u, zg = _project(x_ref, g_ref, w_ref, rc_ref, ra_ref, rb_ref, gsum_ref, gng_ref)
    kt = k.T
    kall_ref[...] = kt
    ktb_ref[...] = kt.astype(BF16)
    for h in range(ATT_HEADS):
        vall_ref[pl.ds(h, tm, stride=ATT_HEADS), :] = v[:, h * V_DIM:(h + 1) * V_DIM]
    qb_ref[...] = (q * (QK_DIM ** -0.5 * LOG2_E)).astype(BF16)
    vb_ref[...] = v.astype(BF16)

    zb = zg.astype(BF16)
    lane = lax.broadcasted_iota(jnp.int32, (CHUNK, LANES), 1)
    wrow = lax.broadcasted_iota(jnp.int32, (CHUNK, 2 * CHUNK), 0)
    wcol = lax.broadcasted_iota(jnp.int32, (CHUNK, 2 * CHUNK), 1)
    tril = (wcol % CHUNK) <= wrow
    bias = bias_ref[...]
    zero = jnp.zeros((CHUNK, LANES), BF16)
    for pr in range(GM_WIDTH // LANES):
        wp = jnp.where(tril, wpair_ref[pr], jnp.zeros((), BF16))
        for c in range(tm // CHUNK):
            rows = slice(c * CHUNK, (c + 1) * CHUNK)
            cols = slice(pr * LANES, (pr + 1) * LANES)
            zp = zb[rows, cols]
            rhs = jnp.concatenate([jnp.where(lane < GM_GROUP_DIM, zp, zero),
                                   jnp.where(lane >= GM_GROUP_DIM, zp, zero)], axis=0)
            mix = _dot(wp, rhs) + bias[:, cols]
            gm_ref[rows, cols] = (u[rows, cols] * mix).astype(BF16)


def _sample_mix_kernel(x_ref, g_ref, w_ref, rc_ref, ra_ref, rb_ref, gsum_ref, gng_ref,
                       coef_ref, bias_ref,
                       q_ref, k_ref, v_ref, zs_ref, gm_ref):
    q, k, v, u, zg = _project(x_ref, g_ref, w_ref, rc_ref, ra_ref, rb_ref, gsum_ref, gng_ref)
    q_ref[...] = q * (QK_DIM ** -0.5)
    k_ref[...] = k
    v_ref[...] = v
    zs_ref[...] = zg
    m, w = zg.shape
    nq = coef_ref.shape[0]
    z3 = zg.reshape(m // nq, nq, w)
    mix = coef_ref[0][None] * z3
    for d in range(1, nq):
        mix = mix + coef_ref[d][None] * pltpu.roll(z3, d, 1)
    mix = mix + bias_ref[...][None]
    gm_ref[...] = (u * mix.reshape(m, w)).astype(BF16)


def _mix_common_specs(layer, tm, n_seq_tiles):
    row = lambda r: (r, 0)
    pos = lambda r: (r % n_seq_tiles, 0)
    return [
        pl.BlockSpec((tm, D_MODEL), row),
        _const_spec((1, D_MODEL)),
        _layer_spec((D_MODEL, IN_COLS), layer),
        pl.BlockSpec((tm, LANES), pos),
        pl.BlockSpec((tm, LANES), pos),
        pl.BlockSpec((tm, LANES), pos),
        _const_spec((GM_WIDTH, GM_WIDTH)),
        _const_spec((1, GM_WIDTH)),
    ]


def _prompt_mix(layer, x2d, g, w_in_b, rope, gsum, gng, wpair, bias, kall, vall, seq):
    m = x2d.shape[0]
    tm = TM_MIX
    n_seq_tiles = seq // tm
    row = lambda r: (r, 0)
    in_specs = _mix_common_specs(layer, tm, n_seq_tiles) + [
        _const_spec(wpair.shape),
        _const_spec(bias.shape),
        pl.BlockSpec(memory_space=pl.ANY),
        pl.BlockSpec(memory_space=pl.ANY),
    ]
    out_specs = [
        pl.BlockSpec((None, None, QK_COLS, tm),
                     lambda r: (layer, r // n_seq_tiles, 0, r % n_seq_tiles)),
        pl.BlockSpec((None, tm * ATT_HEADS, V_DIM), lambda r: (layer, r, 0)),
        pl.BlockSpec((tm, QK_COLS), row),
        pl.BlockSpec((None, QK_COLS, tm), lambda r: (r // n_seq_tiles, 0, r % n_seq_tiles)),
        pl.BlockSpec((tm, ATT_WIDTH), row),
        pl.BlockSpec((tm, GM_WIDTH), row),
    ]
    out_shape = [
        jax.ShapeDtypeStruct(kall.shape, F32),
        jax.ShapeDtypeStruct(vall.shape, F32),
        jax.ShapeDtypeStruct((m, QK_COLS), BF16),
        jax.ShapeDtypeStruct((m // seq, QK_COLS, seq), BF16),
        jax.ShapeDtypeStruct((m, ATT_WIDTH), BF16),
        jax.ShapeDtypeStruct((m, GM_WIDTH), BF16),
    ]
    return pl.pallas_call(
        _prompt_mix_kernel,
        grid=(m // tm,),
        in_specs=in_specs,
        out_specs=out_specs,
        out_shape=out_shape,
        input_output_aliases={10: 0, 11: 1},
        compiler_params=pltpu.CompilerParams(
            dimension_semantics=("arbitrary",), vmem_limit_bytes=VMEM_LIMIT),
        name="prompt_mix",
    )(x2d, g, w_in_b, *rope, gsum, gng, wpair, bias, kall, vall)


def _sample_mix(layer, x2d, g, w_in_b, rope, gsum, gng, coef, bias):
    m = x2d.shape[0]
    row = lambda r: (r, 0)
    in_specs = _mix_common_specs(layer, m, 1) + [
        _const_spec(coef.shape),
        _const_spec(bias.shape),
    ]
    out_specs = [pl.BlockSpec((m, QK_COLS), row)] * 5
    out_shape = [jax.ShapeDtypeStruct((m, QK_COLS), F32)] * 4 + [
        jax.ShapeDtypeStruct((m, GM_WIDTH), BF16)]
    return pl.pallas_call(
        _sample_mix_kernel,
        grid=(1,),
        in_specs=in_specs,
        out_specs=out_specs,
        out_shape=out_shape,
        compiler_params=pltpu.CompilerParams(
            dimension_semantics=("arbitrary",), vmem_limit_bytes=VMEM_LIMIT),
        name="sample_mix",
    )(x2d, g, w_in_b, *rope, gsum, gng, coef, bias)


def _diff_lambda(lq1, lk1, lq2, lk2, lam_init):
    a = jnp.exp(jnp.sum(lq1[...] * lk1[...], axis=-1, keepdims=True))
    b = jnp.exp(jnp.sum(lq2[...] * lk2[...], axis=-1, keepdims=True))
    return a - b + lam_init


def _prompt_attn_kernel(q_ref, kt_ref, v_ref, lq1, lk1, lq2, lk2, sg_ref, o_ref, *, lam_init):
    seq = q_ref.shape[0]
    tq = TQ
    lam = _diff_lambda(lq1, lk1, lq2, lk2, lam_init)
    sg = sg_ref[...]
    lane = lax.broadcasted_iota(jnp.int32, (tq, LANES), 1)
    r = lax.broadcasted_iota(jnp.int32, (tq, tq), 0)
    c = lax.broadcasted_iota(jnp.int32, (tq, tq), 1)
    keep = jnp.concatenate([c <= r, c <= r], axis=0)
    zero = jnp.zeros((tq, LANES), BF16)
    nq = seq // tq
    order = list(range(1, nq, 2)) + list(range(nq - 1 - (nq - 1) % 2, -1, -2))
    assert sorted(order) == list(range(nq))
    for qi in order:
        n = qi * tq
        q = q_ref[n:n + tq, :]
        qs = jnp.concatenate([jnp.where(lane < QK_DIM, q, zero),
                              jnp.where(lane >= QK_DIM, q, zero)], axis=0)
        s_d = jnp.where(keep, _dot(qs, kt_ref[:, n:n + tq]), NEG)
        m = jnp.max(s_d, axis=-1, keepdims=True)
        if qi:
            s_f = _dot(qs, kt_ref[:, :n])
            m = jnp.maximum(m, jnp.max(s_f, axis=-1, keepdims=True))
        p_d = jnp.exp2(s_d - m)
        l = jnp.sum(p_d, axis=-1, keepdims=True)
        o = _dot(p_d.astype(BF16), v_ref[n:n + tq, :])
        if qi:
            p_f = jnp.exp2(s_f - m)
            l = l + jnp.sum(p_f, axis=-1, keepdims=True)
            o = o + _dot(p_f.astype(BF16), v_ref[:n, :])
        o = o / l
        att = o[:tq] - lam * o[tq:]
        o_ref[n:n + tq, :] = (_rms(att, sg) * (1.0 - lam_init)).astype(BF16)


def _prompt_attn(qb, ktb, vb, lams, sg, lam_init, batch, seq):
    m = qb.shape[0]
    small = [_const_spec((1, QK_DIM), single=False)] * 4 + [_const_spec((1, V_DIM), single=False)]
    return pl.pallas_call(
        functools.partial(_prompt_attn_kernel, lam_init=lam_init),
        grid=(batch, ATT_HEADS),
        in_specs=[
            pl.BlockSpec((seq, LANES), lambda b, h: (b, h)),
            pl.BlockSpec((None, LANES, seq), lambda b, h: (b, h, 0)),
            pl.BlockSpec((seq, LANES), lambda b, h: (b, h)),
        ] + small,
        out_specs=pl.BlockSpec((seq, LANES), lambda b, h: (b, h)),
        out_shape=jax.ShapeDtypeStruct((m, ATT_WIDTH), BF16),
        compiler_params=pltpu.CompilerParams(
            dimension_semantics=("arbitrary", "arbitrary"), vmem_limit_bytes=VMEM_LIMIT),
        name="prompt_attn",
    )(qb, ktb, vb, *lams, sg)


def _sample_attn_step(first, last, q_ref, kn_ref, vn_ref, lam_refs, sg_ref, k_refs, v_refs,
                      o_ref, qbd_ref, m_ref, l_ref, acc_ref, lam_init, n_tok):
    rows = qbd_ref.shape[0]
    hrows = rows // ATT_HEADS

    if first:
        qt = jnp.concatenate([q_ref[...]] * (rows // n_tok), axis=0)
        r = lax.broadcasted_iota(jnp.int32, qt.shape, 0)
        c = lax.broadcasted_iota(jnp.int32, qt.shape, 1)
        qbd = jnp.where(r // n_tok == c // QK_DIM, qt, 0.0)
        qbd_ref[...] = qbd.astype(BF16)
        s = _dot_nt(qbd, kn_ref[...])
        rr = lax.broadcasted_iota(jnp.int32, s.shape, 0)
        cc = lax.broadcasted_iota(jnp.int32, s.shape, 1)
        s = jnp.where(cc <= rr % n_tok, s, NEG)
        m0 = jnp.max(s, axis=-1, keepdims=True)
        p = jnp.exp(s - m0)
        m_ref[...] = m0
        l_ref[...] = jnp.sum(p, axis=-1, keepdims=True)
        vn = vn_ref[...]
        acc_ref[...] = jnp.concatenate(
            [_dot(p[h * hrows:(h + 1) * hrows], vn[:, h * V_DIM:(h + 1) * V_DIM])
             for h in range(ATT_HEADS)], axis=0)

    qbd = qbd_ref[...]
    s = jnp.concatenate([_dot(qbd, kr[...].astype(BF16)) for kr in k_refs], axis=1)
    m_old = m_ref[...]
    m_new = jnp.maximum(m_old, jnp.max(s, axis=-1, keepdims=True))
    alpha = jnp.exp(m_old - m_new)
    p = jnp.exp(s - m_new)
    l_ref[...] = alpha * l_ref[...] + jnp.sum(p, axis=-1, keepdims=True)
    pb = p.astype(BF16)
    parts = []
    for h in range(ATT_HEADS):
        vh = jnp.concatenate(
            [vr[pl.ds(h, PAGE_SIZE, stride=ATT_HEADS), :].astype(BF16) for vr in v_refs], axis=0)
        parts.append(_dot(pb[h * hrows:(h + 1) * hrows], vh))
    acc_ref[...] = alpha * acc_ref[...] + jnp.concatenate(parts, axis=0)
    m_ref[...] = m_new

    if last:
        o = acc_ref[...] / l_ref[...]
        lam = _diff_lambda(*lam_refs, lam_init)
        sg = sg_ref[...]
        for h in range(ATT_HEADS):
            r0 = h * hrows
            att = o[r0:r0 + n_tok] - lam * o[r0 + n_tok:r0 + 2 * n_tok]
            o_ref[:, h * V_DIM:(h + 1) * V_DIM] = _rms(att, sg) * (1.0 - lam_init)


def _ffn_pre(x_ref, att_ref, gm_ref, wo_ref, nf_ref, x1_ref, hn_ref, acc_ref):
    mix = jnp.concatenate([att_ref[...].astype(BF16), gm_ref[...]], axis=1)
    x1 = x_ref[...] + _dot(mix, wo_ref[...])
    x1_ref[...] = x1
    hn_ref[...] = _rms(x1, nf_ref[...]).astype(BF16)
    acc_ref[...] = jnp.zeros(acc_ref.shape, F32)


def _ffn_chunk(chunks, hn_ref, w1_ref, cw_ref, cb_ref, w2_ref, acc_ref, conv_fn):
    cols = slice(chunks[0] * FF_CHUNK, (chunks[-1] + 1) * FF_CHUNK)
    gcols = slice(D_FF + cols.start, D_FF + cols.stop)
    hn = hn_ref[...]
    a = _dot(hn, w1_ref[:, cols])
    gate_in = _dot(hn, w1_ref[:, gcols])
    cw = cw_ref[:, cols]
    a1, a2 = conv_fn(cols, a)
    ac = cb_ref[:, cols] + cw[0:1] * a2 + cw[1:2] * a1 + cw[2:3] * a
    hid = (ac * _sigmoid(ac)) * gate_in
    acc_ref[...] += _dot(hid.astype(BF16), w2_ref[cols, :])


def _ffn_post(x1_ref, acc_ref, p_ref, np_ref, wg_ref, wp_ref, nfin_ref, y_ref, final):
    x2 = x1_ref[...] + acc_ref[...]
    gate = _sigmoid(_dot(_rms(x2, np_ref[...]).astype(BF16), wg_ref[...]))
    x3 = x2 + gate * _dot(p_ref[...].astype(BF16), wp_ref[...])
    y_ref[...] = _rms(x3, nfin_ref[...]) if final else x3


def _split_chunks(n_parts):
    base, extra = divmod(N_FF_CHUNKS, n_parts)
    out, start = [], 0
    for i in range(n_parts):
        size = base + (1 if i < extra else 0)
        out.append(range(start, start + size))
        start += size
    return out


def _prompt_finish_kernel(pt_ref, x_ref, att_ref, gm_ref, p_ref, wo_ref, nf_ref, w1_ref, cw_ref,
                          cb_ref, w2_ref, np_ref, wg_ref, wp_ref, nfin_ref,
                          q_ref, kn_ref, vn_ref, lq1, lk1, lq2, lk2, sg_ref, ck_hbm, cv_hbm,
                          y_ref, cs_ref, atts_ref,
                          hn_ref, acc_ref, carry_ref, qbd_ref, m_ref, l_ref, accs_ref,
                          kbuf, vbuf, sem, *, layer, final, lam_init, n_tok, n_sub):
    npg = PAGES_PER_STEP
    x1_ref = y_ref
    s = pl.program_id(1)
    k = pl.program_id(2)
    tm = x_ref.shape[0]
    tile = pl.program_id(0) * pl.num_programs(1) + s
    n_tiles = pl.num_programs(0) * pl.num_programs(1)

    def page_copy(is_key, slot, i, page):
        src, dst = (ck_hbm, kbuf) if is_key else (cv_hbm, vbuf)
        return pltpu.make_async_copy(src.at[layer, page], dst.at[slot, i],
                                     sem.at[slot, 0 if is_key else 1])

    def start_pages(slot, row, sub):
        for i in range(npg):
            page = pt_ref[row, sub * npg + i]
            page_copy(True, slot, i, page).start()
            page_copy(False, slot, i, page).start()

    def wait_pages(slot):
        for i in range(npg):
            page_copy(True, slot, i, 0).wait()
            page_copy(False, slot, i, 0).wait()

    def conv_fn(cols, a):
        prev = carry_ref[:, cols]
        row = lax.broadcasted_iota(jnp.int32, a.shape, 0)
        a1 = jnp.where(row == 0, prev[1:2], pltpu.roll(a, 1, 0))
        a2 = jnp.where(row == 0, prev[0:1], jnp.where(row == 1, prev[1:2], pltpu.roll(a, 2, 0)))
        tail = a[tm - (CONV_W - 1):, :]
        carry_ref[:, cols] = tail
        cs_ref[:, cols] = tail
        return a1, a2

    @pl.when(jnp.logical_and(s == 0, k == 0))
    def _():
        carry_ref[...] = jnp.zeros(carry_ref.shape, F32)

    @pl.when(jnp.logical_and(tile == 0, k == 0))
    def _():
        start_pages(0, 0, 0)

    for sub, chunks in enumerate(_split_chunks(n_sub)):
        @pl.when(k == sub)
        def _(sub=sub, chunks=chunks):
            slot = sub % 2
            wait_pages(slot)
            if sub < n_sub - 1:
                start_pages(1 - slot, tile, sub + 1)
            else:
                @pl.when(tile + 1 < n_tiles)
                def _():
                    start_pages(1 - slot, tile + 1, 0)
            k_refs = [kbuf.at[slot, i] for i in range(npg)]
            v_refs = [vbuf.at[slot, i] for i in range(npg)]
            _sample_attn_step(sub == 0, sub == n_sub - 1, q_ref, kn_ref, vn_ref,
                              (lq1, lk1, lq2, lk2), sg_ref, k_refs, v_refs, atts_ref,
                              qbd_ref, m_ref, l_ref, accs_ref, lam_init, n_tok)
            if sub == 0:
                _ffn_pre(x_ref, att_ref, gm_ref, wo_ref, nf_ref, x1_ref, hn_ref, acc_ref)
            _ffn_chunk(chunks, hn_ref, w1_ref, cw_ref, cb_ref, w2_ref, acc_ref, conv_fn)
            if sub == n_sub - 1:
                _ffn_post(x1_ref, acc_ref, p_ref, np_ref, wg_ref, wp_ref, nfin_ref, y_ref, final)


def _sample_finish_kernel(x_ref, att_ref, gm_ref, p_ref, wo_ref, nf_ref, w1_ref, cw_ref,
                          cb_ref, w2_ref, np_ref, wg_ref, wp_ref, nfin_ref, st_ref,
                          y_ref, cs_ref, hn_ref, acc_ref, *, final, n_tok):
    x1_ref = y_ref

    def conv_fn(cols, a):
        m, w = a.shape
        a3 = a.reshape(m // n_tok, n_tok, w)
        prev = st_ref[:, :, cols]
        t = lax.broadcasted_iota(jnp.int32, a3.shape, 1)
        p0, p1 = prev[:, 0:1, :], prev[:, 1:2, :]
        a1 = jnp.where(t == 0, p1, pltpu.roll(a3, 1, 1))
        a2 = jnp.where(t == 0, p0, jnp.where(t == 1, p1, pltpu.roll(a3, 2, 1)))
        cs_ref[:, :, cols] = a3[:, n_tok - (CONV_W - 1):, :]
        return a1.reshape(m, w), a2.reshape(m, w)

    _ffn_pre(x_ref, att_ref, gm_ref, wo_ref, nf_ref, x1_ref, hn_ref, acc_ref)
    for chunks in _split_chunks(SAMPLE_FFN_SLABS):
        _ffn_chunk(chunks, hn_ref, w1_ref, cw_ref, cb_ref, w2_ref, acc_ref, conv_fn)
    _ffn_post(x1_ref, acc_ref, p_ref, np_ref, wg_ref, wp_ref, nfin_ref, y_ref, final)


def _finish_weight_specs(layer):
    return [
        _layer_spec((D_MODEL, D_MODEL), layer),
        _const_spec((1, D_MODEL)),
        _layer_spec((D_MODEL, 2 * D_FF), layer),
        _const_spec((CONV_W, D_FF)),
        _const_spec((1, D_FF)),
        _layer_spec((D_FF, D_MODEL), layer),
        _const_spec((1, D_MODEL)),
        _layer_spec((D_MODEL, D_MODEL), layer),
        _layer_spec((PLE_DIM, D_MODEL), layer),
        _const_spec((1, D_MODEL)),
    ]


def _finish_scratch(tm):
    return [
        pltpu.VMEM((tm, D_MODEL), BF16),
        pltpu.VMEM((tm, D_MODEL), F32),
    ]


def _prompt_finish(layer, x2d, att, gm, p2d, weights, batch, seq, final,
                   page_table, q_s, kn_s, vn_s, lams, sg, cache_kt, cache_vr, lam_init, n_tok):
    m = x2d.shape[0]
    tm = TM_FIN
    ns = seq // tm
    npg = PAGES_PER_STEP
    bs, n_pages = page_table.shape
    n_sub = n_pages // npg
    assert bs == batch * ns and n_pages == n_sub * npg
    assert n_sub <= N_FF_CHUNKS and n_sub % 2 == 0
    rows = ATT_HEADS * 2 * n_tok
    row = lambda b, s, k, pt: (b * ns + s, 0)
    fixed = lambda *_: (0, 0)
    in_specs = [
        pl.BlockSpec((tm, D_MODEL), row),
        pl.BlockSpec((tm, ATT_WIDTH), row),
        pl.BlockSpec((tm, GM_WIDTH), row),
        pl.BlockSpec((tm, PLE_DIM), row),
    ] + _finish_weight_specs(layer) + [
        pl.BlockSpec((n_tok, QK_COLS), row),
        pl.BlockSpec((n_tok, QK_COLS), row),
        pl.BlockSpec((n_tok, ATT_WIDTH), row),
    ] + [pl.BlockSpec((1, QK_DIM), fixed)] * 4 + [pl.BlockSpec((1, V_DIM), fixed)] + [
        pl.BlockSpec(memory_space=pl.ANY),
        pl.BlockSpec(memory_space=pl.ANY),
    ]
    page_buf = pltpu.VMEM((2, npg) + cache_kt.shape[2:], F32)
    grid_spec = pltpu.PrefetchScalarGridSpec(
        num_scalar_prefetch=1,
        grid=(batch, ns, n_sub),
        in_specs=in_specs,
        out_specs=[
            pl.BlockSpec((tm, D_MODEL), row),
            pl.BlockSpec((None, CONV_W - 1, D_FF), lambda b, s, k, pt: (b, 0, 0)),
            pl.BlockSpec((n_tok, ATT_WIDTH), row),
        ],
        scratch_shapes=_finish_scratch(tm) + [
            pltpu.VMEM((CONV_W - 1, D_FF), F32),
            pltpu.VMEM((rows, QK_COLS), BF16),
            pltpu.VMEM((rows, 1), F32),
            pltpu.VMEM((rows, 1), F32),
            pltpu.VMEM((rows, V_DIM), F32),
            page_buf,
            page_buf,
            pltpu.SemaphoreType.DMA((2, 2)),
        ],
    )
    return pl.pallas_call(
        functools.partial(_prompt_finish_kernel, layer=layer, final=final, lam_init=lam_init,
                          n_tok=n_tok, n_sub=n_sub),
        grid_spec=grid_spec,
        out_shape=[
            jax.ShapeDtypeStruct((m, D_MODEL), F32),
            jax.ShapeDtypeStruct((batch, CONV_W - 1, D_FF), F32),
            jax.ShapeDtypeStruct((bs * n_tok, ATT_WIDTH), F32),
        ],
        compiler_params=pltpu.CompilerParams(
            dimension_semantics=("arbitrary", "arbitrary", "arbitrary"),
            vmem_limit_bytes=VMEM_LIMIT),
        name="prompt_finish",
    )(page_table, x2d, att, gm, p2d, *weights, q_s, kn_s, vn_s, *lams, sg, cache_kt, cache_vr)


def _sample_finish(layer, x2d, att, gm, p2d, weights, state, n_tok, final):
    m = x2d.shape[0]
    row = lambda i: (0, 0)
    in_specs = [
        pl.BlockSpec((m, D_MODEL), row),
        pl.BlockSpec((m, ATT_WIDTH), row),
        pl.BlockSpec((m, GM_WIDTH), row),
        pl.BlockSpec((m, PLE_DIM), row),
    ] + _finish_weight_specs(layer) + [_const_spec(state.shape)]
    return pl.pallas_call(
        functools.partial(_sample_finish_kernel, final=final, n_tok=n_tok),
        grid=(1,),
        in_specs=in_specs,
        out_specs=[
            pl.BlockSpec((m, D_MODEL), row),
            pl.BlockSpec(state.shape, lambda i: (0, 0, 0)),
        ],
        out_shape=[
            jax.ShapeDtypeStruct((m, D_MODEL), F32),
            jax.ShapeDtypeStruct(state.shape, F32),
        ],
        scratch_shapes=_finish_scratch(m),
        compiler_params=pltpu.CompilerParams(
            dimension_semantics=("arbitrary",), vmem_limit_bytes=VMEM_LIMIT),
        name="sample_finish",
    )(x2d, att, gm, p2d, *weights, state)


def _rope_tables(pos):
    half = ROT_DIM // 2
    inv = ROPE_THETA ** (-jnp.arange(half, dtype=F32) * 2.0 / ROT_DIM)
    ang = pos.astype(F32)[:, None] * inv[None, :]
    cos, sin = jnp.cos(ang), jnp.sin(ang)
    n = pos.shape[0]
    rest = QK_DIM - ROT_DIM
    rc = jnp.concatenate([cos, cos, jnp.ones((n, rest), F32)], axis=1)
    ra = jnp.concatenate([-sin, jnp.zeros((n, half + rest), F32)], axis=1)
    rb = jnp.concatenate([jnp.zeros((n, half), F32), sin, jnp.zeros((n, rest), F32)], axis=1)
    rep = LANES // QK_DIM
    return tuple(jnp.tile(t, (1, rep)) for t in (rc, ra, rb))


def kernel(x_prompt, x_sample, cache_k, cache_v, state_ffn_conv, page_table, p_prompt, p_sample,
           norm_mix, w_in, lambda_q1, lambda_k1, lambda_q2, lambda_k2, subln_g, gm_norm_g,
           gm_w_s, gm_b_s, w_out, norm_ffn, w_ffn_in, conv_w, conv_b, w_ffn_out, norm_ple,
           w_ple_gate, w_ple_proj, norm_final):
    bp, seq, _ = x_prompt.shape
    bs, n_tok, _ = x_sample.shape
    n_phys = cache_k.shape[1]
    n_pages = page_table.shape[1]
    past_len = n_pages * PAGE_SIZE
    mp, ms = bp * seq, bs * n_tok

    w_in_b = w_in.astype(BF16)
    w_out_b = w_out.astype(BF16)
    w1_b = w_ffn_in.astype(BF16)
    w2_b = w_ffn_out.astype(BF16)
    wg_b = w_ple_gate.astype(BF16)
    wp_b = w_ple_proj.astype(BF16)

    rope_p = _rope_tables(jnp.arange(seq))
    rope_s = _rope_tables(past_len + jnp.arange(n_tok))
    rope_s = tuple(jnp.tile(t, (bs, 1)) for t in rope_s)
    gsum = jnp.kron(jnp.eye(GM_GROUPS, dtype=F32),
                    jnp.ones((GM_GROUP_DIM, GM_GROUP_DIM), F32)).astype(BF16)

    wpair = gm_w_s.reshape(DEPTH, GM_GROUPS // 2, 2, CHUNK, CHUNK).transpose(0, 1, 3, 2, 4)
    wpair = wpair.reshape(DEPTH, GM_GROUPS // 2, CHUNK, 2 * CHUNK).astype(BF16)
    bias_p = jnp.repeat(gm_b_s.transpose(0, 2, 1), GM_GROUP_DIM, axis=2)
    t_idx = jnp.arange(n_tok)[:, None]
    d_idx = jnp.arange(n_tok)[None, :]
    src = t_idx - d_idx
    w_small = jnp.tril(gm_w_s[:, :, :n_tok, :n_tok])
    coef = jnp.where(src >= 0, w_small[:, :, t_idx, jnp.maximum(src, 0)], 0.0)
    coef = jnp.repeat(coef.transpose(0, 3, 2, 1), GM_GROUP_DIM, axis=3)
    bias_s = jnp.repeat(gm_b_s[:, :, :n_tok].transpose(0, 2, 1), GM_GROUP_DIM, axis=2)

    cache_kt = cache_k.transpose(0, 1, 3, 4, 5, 2).reshape(DEPTH, n_phys, QK_COLS, PAGE_SIZE)
    cache_vr = cache_v.reshape(DEPTH, n_phys, PAGE_SIZE * ATT_HEADS, V_DIM)

    xp = x_prompt.reshape(mp, D_MODEL)
    xs = x_sample.reshape(ms, D_MODEL)
    pp = p_prompt.reshape(DEPTH, mp, PLE_DIM)
    ps = p_sample.reshape(DEPTH, ms, PLE_DIM)
    kall = jnp.zeros((DEPTH, bp, QK_COLS, seq), F32)
    vall = jnp.zeros((DEPTH, mp * ATT_HEADS, V_DIM), F32)
    nfin = norm_final[None, :]
    cp_l, ks_l, vs_l, cs_l, zs_l = [], [], [], [], []
    for i in range(DEPTH):
        lam_init = 0.8 - 0.6 * math.exp(-0.3 * i)
        final = i == DEPTH - 1
        lams = (lambda_q1[i][None], lambda_k1[i][None], lambda_q2[i][None], lambda_k2[i][None])
        sg = subln_g[i][None]
        gmix = norm_mix[i][None]
        gng = gm_norm_g[i][None]
        weights = (w_out_b, norm_ffn[i][None], w1_b, conv_w[i], conv_b[i][None], w2_b,
                   norm_ple[i][None], wg_b, wp_b, nfin)

        kall, vall, qb, kb, vb, gm_p = _prompt_mix(
            i, xp, gmix, w_in_b, rope_p, gsum, gng, wpair[i], bias_p[i], kall, vall, seq)
        att_p = _prompt_attn(qb, kb, vb, lams, sg, lam_init, bp, seq)
        q_s, k_s, v_s, z_s, gm_s = _sample_mix(
            i, xs, gmix, w_in_b, rope_s, gsum, gng, coef[i], bias_s[i])
        xp, cp, att_s = _prompt_finish(
            i, xp, att_p, gm_p, pp[i], weights, bp, seq, final,
            page_table, q_s, k_s, v_s, lams, sg, cache_kt, cache_vr, lam_init, n_tok)
        xs, cs = _sample_finish(i, xs, att_s, gm_s, ps[i], weights, state_ffn_conv[i], n_tok,
                                final)
        cp_l.append(cp)
        ks_l.append(k_s)
        vs_l.append(v_s)
        zs_l.append(z_s)
        cs_l.append(cs)

    return (
        xp.reshape(bp, seq, D_MODEL),
        xs.reshape(bs, n_tok, D_MODEL),
        kall.reshape(DEPTH, bp, ATT_HEADS, 2, QK_DIM, seq).transpose(0, 1, 5, 2, 3, 4),
        vall.reshape(DEPTH, bp, seq, ATT_HEADS, V_DIM),
        jnp.stack(cp_l),
        jnp.stack(ks_l).reshape(DEPTH, bs, n_tok, ATT_HEADS, 2, QK_DIM),
        jnp.stack(vs_l).reshape(DEPTH, bs, n_tok, ATT_HEADS, V_DIM),
        jnp.stack(cs_l),
        jnp.stack(zs_l).reshape(DEPTH, bs, n_tok, GM_WIDTH),
    )
```

```python
import functools
import math

import jax
import jax.numpy as jnp
from jax import lax
from jax.experimental import pallas as pl
from jax.experimental.pallas import tpu as pltpu

D_MODEL = 1024
DEPTH = 4
PAGE_SIZE = 128
ATT_HEADS = 4
QK_DIM = 64
V_DIM = 2 * QK_DIM
ATT_WIDTH = ATT_HEADS * V_DIM
QK_COLS = ATT_HEADS * 2 * QK_DIM
GM_WIDTH = D_MODEL - ATT_WIDTH
GM_GROUPS = 8
GM_GROUP_DIM = GM_WIDTH // GM_GROUPS
CHUNK = 128
ROT_DIM = QK_DIM // 4
ROPE_THETA = 500000.0
IN_COLS = 2 * QK_COLS + ATT_WIDTH + 2 * GM_WIDTH
D_FF = 2816
CONV_W = 3
PLE_DIM = 256
EPS = 1e-6
NEG = -1e30
LOG2_E = math.log2(math.e)

LANES = 128
FF_CHUNK = 256
N_FF_CHUNKS = D_FF // FF_CHUNK
VMEM_LIMIT = 56 * 1024 * 1024

SAMPLE_FFN_SLABS = 4
TM_MIX = 1024
TM_FIN = 512
TQ = 256
PAGES_PER_STEP = 16

F32 = jnp.float32
BF16 = jnp.bfloat16


def _rms(x, g):
    r = lax.rsqrt(jnp.mean(x * x, axis=-1, keepdims=True) + EPS)
    return (x * r) * g


def _sigmoid(x):
    return 1.0 / (1.0 + jnp.exp(-x))


def _dot(a, b):
    return jnp.dot(a, b, preferred_element_type=F32)


def _dot_nt(a, b):
    return lax.dot_general(a, b, (((1,), (1,)), ((), ())), preferred_element_type=F32)


def _const_spec(shape, single=True):
    nd = len(shape)
    idx = lambda *_: (0,) * nd
    if single:
        return pl.BlockSpec(shape, idx, pipeline_mode=pl.Buffered(1))
    return pl.BlockSpec(shape, idx)


def _layer_spec(shape, layer):
    nd = len(shape)
    return pl.BlockSpec((None,) + tuple(shape), lambda *_: (layer,) + (0,) * nd,
                        pipeline_mode=pl.Buffered(1))


def _project(x_ref, g_ref, w_ref, rc_ref, ra_ref, rb_ref, gsum_ref, gng_ref):
    h = _rms(x_ref[...], g_ref[...]).astype(BF16)
    z = _dot(h, w_ref[...])
    rc, ra, rb = rc_ref[...], ra_ref[...], rb_ref[...]

    def rope(t):
        cols = []
        for j in range(QK_COLS // LANES):
            tj = t[:, j * LANES:(j + 1) * LANES]
            up = pltpu.roll(tj, LANES - ROT_DIM // 2, 1)
            dn = pltpu.roll(tj, ROT_DIM // 2, 1)
            cols.append(tj * rc + up * ra + dn * rb)
        return jnp.concatenate(cols, axis=1)

    q = rope(z[:, :QK_COLS])
    k = rope(z[:, QK_COLS:2 * QK_COLS])
    o = 2 * QK_COLS
    v = z[:, o:o + ATT_WIDTH]
    o += ATT_WIDTH
    u = z[:, o:o + GM_WIDTH]
    vg = z[:, o + GM_WIDTH:]
    ss = _dot((vg * vg).astype(BF16), gsum_ref[...]) * (1.0 / GM_GROUP_DIM)
    zg = (vg * lax.rsqrt(ss + EPS)) * gng_ref[...]
    return q, k, v, u, zg


def _prompt_mix_kernel(x_ref, g_ref, w_ref, rc_ref, ra_ref, rb_ref, gsum_ref, gng_ref,
                       wpair_ref, bias_ref, kall_in, vall_in,
                       kall_ref, vall_ref, qb_ref, ktb_ref, vb_ref, gm_ref):
    del kall_in, vall_in
    tm = x_ref.shape[0]
    q, k, v, u, zg = _project(x_ref, g_ref, w_ref, rc_ref, ra_ref, rb_ref, gsum_ref, gng_ref)
    kt = k.T
    kall_ref[...] = kt
    ktb_ref[...] = kt.astype(BF16)
    for h in range(ATT_HEADS):
        vall_ref[pl.ds(h, tm, stride=ATT_HEADS), :] = v[:, h * V_DIM:(h + 1) * V_DIM]
    qb_ref[...] = (q * (QK_DIM ** -0.5 * LOG2_E)).astype(BF16)
    vb_ref[...] = v.astype(BF16)

    zb = zg.astype(BF16)
    lane = lax.broadcasted_iota(jnp.int32, (CHUNK, LANES), 1)
    wrow = lax.broadcasted_iota(jnp.int32, (CHUNK, 2 * CHUNK), 0)
    wcol = lax.broadcasted_iota(jnp.int32, (CHUNK, 2 * CHUNK), 1)
    tril = (wcol % CHUNK) <= wrow
    bias = bias_ref[...]
    zero = jnp.zeros((CHUNK, LANES), BF16)
    for pr in range(GM_WIDTH // LANES):
        wp = jnp.where(tril, wpair_ref[pr], jnp.zeros((), BF16))
        for c in range(tm // CHUNK):
            rows = slice(c * CHUNK, (c + 1) * CHUNK)
            cols = slice(pr * LANES, (pr + 1) * LANES)
            zp = zb[rows, cols]
            rhs = jnp.concatenate([jnp.where(lane < GM_GROUP_DIM, zp, zero),
                                   jnp.where(lane >= GM_GROUP_DIM, zp, zero)], axis=0)
            mix = _dot(wp, rhs) + bias[:, cols]
            gm_ref[rows, cols] = (u[rows, cols] * mix).astype(BF16)


def _sample_mix_kernel(x_ref, g_ref, w_ref, rc_ref, ra_ref, rb_ref, gsum_ref, gng_ref,
                       coef_ref, bias_ref,
                       q_ref, k_ref, v_ref, zs_ref, gm_ref):
    q, k, v, u, zg = _project(x_ref, g_ref, w_ref, rc_ref, ra_ref, rb_ref, gsum_ref, gng_ref)
    q_ref[...] = q * (QK_DIM ** -0.5)
    k_ref[...] = k
    v_ref[...] = v
    zs_ref[...] = zg
    m, w = zg.shape
    nq = coef_ref.shape[0]
    z3 = zg.reshape(m // nq, nq, w)
    mix = coef_ref[0][None] * z3
    for d in range(1, nq):
        mix = mix + coef_ref[d][None] * pltpu.roll(z3, d, 1)
    mix = mix + bias_ref[...][None]
    gm_ref[...] = (u * mix.reshape(m, w)).astype(BF16)


def _mix_common_specs(layer, tm, n_seq_tiles):
    row = lambda r: (r, 0)
    pos = lambda r: (r % n_seq_tiles, 0)
    return [
        pl.BlockSpec((tm, D_MODEL), row),
        _const_spec((1, D_MODEL)),
        _layer_spec((D_MODEL, IN_COLS), layer),
        pl.BlockSpec((tm, LANES), pos),
        pl.BlockSpec((tm, LANES), pos),
        pl.BlockSpec((tm, LANES), pos),
        _const_spec((GM_WIDTH, GM_WIDTH)),
        _const_spec((1, GM_WIDTH)),
    ]


def _prompt_mix(layer, x2d, g, w_in_b, rope, gsum, gng, wpair, bias, kall, vall, seq):
    m = x2d.shape[0]
    tm = TM_MIX
    n_seq_tiles = seq // tm
    row = lambda r: (r, 0)
    in_specs = _mix_common_specs(layer, tm, n_seq_tiles) + [
        _const_spec(wpair.shape),
        _const_spec(bias.shape),
        pl.BlockSpec(memory_space=pl.ANY),
        pl.BlockSpec(memory_space=pl.ANY),
    ]
    out_specs = [
        pl.BlockSpec((None, None, QK_COLS, tm),
                     lambda r: (layer, r // n_seq_tiles, 0, r % n_seq_tiles)),
        pl.BlockSpec((None, tm * ATT_HEADS, V_DIM), lambda r: (layer, r, 0)),
        pl.BlockSpec((tm, QK_COLS), row),
        pl.BlockSpec((None, QK_COLS, tm), lambda r: (r // n_seq_tiles, 0, r % n_seq_tiles)),
        pl.BlockSpec((tm, ATT_WIDTH), row),
        pl.BlockSpec((tm, GM_WIDTH), row),
    ]
    out_shape = [
        jax.ShapeDtypeStruct(kall.shape, F32),
        jax.ShapeDtypeStruct(vall.shape, F32),
        jax.ShapeDtypeStruct((m, QK_COLS), BF16),
        jax.ShapeDtypeStruct((m // seq, QK_COLS, seq), BF16),
        jax.ShapeDtypeStruct((m, ATT_WIDTH), BF16),
        jax.ShapeDtypeStruct((m, GM_WIDTH), BF16),
    ]
    return pl.pallas_call(
        _prompt_mix_kernel,
        grid=(m // tm,),
        in_specs=in_specs,
        out_specs=out_specs,
        out_shape=out_shape,
        input_output_aliases={10: 0, 11: 1},
        compiler_params=pltpu.CompilerParams(
            dimension_semantics=("arbitrary",), vmem_limit_bytes=VMEM_LIMIT),
        name="prompt_mix",
    )(x2d, g, w_in_b, *rope, gsum, gng, wpair, bias, kall, vall)


def _sample_mix(layer, x2d, g, w_in_b, rope, gsum, gng, coef, bias):
    m = x2d.shape[0]
    row = lambda r: (r, 0)
    in_specs = _mix_common_specs(layer, m, 1) + [
        _const_spec(coef.shape),
        _const_spec(bias.shape),
    ]
    out_specs = [pl.BlockSpec((m, QK_COLS), row)] * 5
    out_shape = [jax.ShapeDtypeStruct((m, QK_COLS), F32)] * 4 + [
        jax.ShapeDtypeStruct((m, GM_WIDTH), BF16)]
    return pl.pallas_call(
        _sample_mix_kernel,
        grid=(1,),
        in_specs=in_specs,
        out_specs=out_specs,
        out_shape=out_shape,
        compiler_params=pltpu.CompilerParams(
            dimension_semantics=("arbitrary",), vmem_limit_bytes=VMEM_LIMIT),
        name="sample_mix",
    )(x2d, g, w_in_b, *rope, gsum, gng, coef, bias)


def _diff_lambda(lq1, lk1, lq2, lk2, lam_init):
    a = jnp.exp(jnp.sum(lq1[...] * lk1[...], axis=-1, keepdims=True))
    b = jnp.exp(jnp.sum(lq2[...] * lk2[...], axis=-1, keepdims=True))
    return a - b + lam_init


def _prompt_attn_kernel(q_ref, kt_ref, v_ref, lq1, lk1, lq2, lk2, sg_ref, o_ref, *, lam_init):
    seq = q_ref.shape[0]
    tq = TQ
    lam = _diff_lambda(lq1, lk1, lq2, lk2, lam_init)
    sg = sg_ref[...]
    lane = lax.broadcasted_iota(jnp.int32, (tq, LANES), 1)
    r = lax.broadcasted_iota(jnp.int32, (tq, tq), 0)
    c = lax.broadcasted_iota(jnp.int32, (tq, tq), 1)
    keep = jnp.concatenate([c <= r, c <= r], axis=0)
    zero = jnp.zeros((tq, LANES), BF16)
    nq = seq // tq
    order = list(range(1, nq, 2)) + list(range(nq - 1 - (nq - 1) % 2, -1, -2))
    assert sorted(order) == list(range(nq))
    for qi in order:
        n = qi * tq
        q = q_ref[n:n + tq, :]
        qs = jnp.concatenate([jnp.where(lane < QK_DIM, q, zero),
                              jnp.where(lane >= QK_DIM, q, zero)], axis=0)
        s_d = jnp.where(keep, _dot(qs, kt_ref[:, n:n + tq]), NEG)
        m = jnp.max(s_d, axis=-1, keepdims=True)
        if qi:
            s_f = _dot(qs, kt_ref[:, :n])
            m = jnp.maximum(m, jnp.max(s_f, axis=-1, keepdims=True))
        p_d = jnp.exp2(s_d - m)
        l = jnp.sum(p_d, axis=-1, keepdims=True)
        o = _dot(p_d.astype(BF16), v_ref[n:n + tq, :])
        if qi:
            p_f = jnp.exp2(s_f - m)
            l = l + jnp.sum(p_f, axis=-1, keepdims=True)
            o = o + _dot(p_f.astype(BF16), v_ref[:n, :])
        o = o / l
        att = o[:tq] - lam * o[tq:]
        o_ref[n:n + tq, :] = (_rms(att, sg) * (1.0 - lam_init)).astype(BF16)


def _prompt_attn(qb, ktb, vb, lams, sg, lam_init, batch, seq):
    m = qb.shape[0]
    small = [_const_spec((1, QK_DIM), single=False)] * 4 + [_const_spec((1, V_DIM), single=False)]
    return pl.pallas_call(
        functools.partial(_prompt_attn_kernel, lam_init=lam_init),
        grid=(batch, ATT_HEADS),
        in_specs=[
            pl.BlockSpec((seq, LANES), lambda b, h: (b, h)),
            pl.BlockSpec((None, LANES, seq), lambda b, h: (b, h, 0)),
            pl.BlockSpec((seq, LANES), lambda b, h: (b, h)),
        ] + small,
        out_specs=pl.BlockSpec((seq, LANES), lambda b, h: (b, h)),
        out_shape=jax.ShapeDtypeStruct((m, ATT_WIDTH), BF16),
        compiler_params=pltpu.CompilerParams(
            dimension_semantics=("arbitrary", "arbitrary"), vmem_limit_bytes=VMEM_LIMIT),
        name="prompt_attn",
    )(qb, ktb, vb, *lams, sg)


def _sample_attn_step(first, last, q_ref, kn_ref, vn_ref, lam_refs, sg_ref, k_refs, v_refs,
                      o_ref, qbd_ref, m_ref, l_ref, acc_ref, lam_init, n_tok):
    rows = qbd_ref.shape[0]
    hrows = rows // ATT_HEADS

    if first:
        qt = jnp.concatenate([q_ref[...]] * (rows // n_tok), axis=0)
        r = lax.broadcasted_iota(jnp.int32, qt.shape, 0)
        c = lax.broadcasted_iota(jnp.int32, qt.shape, 1)
        qbd = jnp.where(r // n_tok == c // QK_DIM, qt, 0.0)
        qbd_ref[...] = qbd.astype(BF16)
        s = _dot_nt(qbd, kn_ref[...])
        rr = lax.broadcasted_iota(jnp.int32, s.shape, 0)
        cc = lax.broadcasted_iota(jnp.int32, s.shape, 1)
        s = jnp.where(cc <= rr % n_tok, s, NEG)
        m0 = jnp.max(s, axis=-1, keepdims=True)
        p = jnp.exp(s - m0)
        m_ref[...] = m0
        l_ref[...] = jnp.sum(p, axis=-1, keepdims=True)
        vn = vn_ref[...]
        acc_ref[...] = jnp.concatenate(
            [_dot(p[h * hrows:(h + 1) * hrows], vn[:, h * V_DIM:(h + 1) * V_DIM])
             for h in range(ATT_HEADS)], axis=0)

    qbd = qbd_ref[...]
    s = jnp.concatenate([_dot(qbd, kr[...].astype(BF16)) for kr in k_refs], axis=1)
    m_old = m_ref[...]
    m_new = jnp.maximum(m_old, jnp.max(s, axis=-1, keepdims=True))
    alpha = jnp.exp(m_old - m_new)
    p = jnp.exp(s - m_new)
    l_ref[...] = alpha * l_ref[...] + jnp.sum(p, axis=-1, keepdims=True)
    pb = p.astype(BF16)
    parts = []
    for h in range(ATT_HEADS):
        vh = jnp.concatenate(
            [vr[pl.ds(h, PAGE_SIZE, stride=ATT_HEADS), :].astype(BF16) for vr in v_refs], axis=0)
        parts.append(_dot(pb[h * hrows:(h + 1) * hrows], vh))
    acc_ref[...] = alpha * acc_ref[...] + jnp.concatenate(parts, axis=0)
    m_ref[...] = m_new

    if last:
        o = acc_ref[...] / l_ref[...]
        lam = _diff_lambda(*lam_refs, lam_init)
        sg = sg_ref[...]
        for h in range(ATT_HEADS):
            r0 = h * hrows
            att = o[r0:r0 + n_tok] - lam * o[r0 + n_tok:r0 + 2 * n_tok]
            o_ref[:, h * V_DIM:(h + 1) * V_DIM] = _rms(att, sg) * (1.0 - lam_init)


def _ffn_pre(x_ref, att_ref, gm_ref, wo_ref, nf_ref, x1_ref, hn_ref, acc_ref):
    mix = jnp.concatenate([att_ref[...].astype(BF16), gm_ref[...]], axis=1)
    x1 = x_ref[...] + _dot(mix, wo_ref[...])
    x1_ref[...] = x1
    hn_ref[...] = _rms(x1, nf_ref[...]).astype(BF16)
    acc_ref[...] = jnp.zeros(acc_ref.shape, F32)


def _ffn_chunk(chunks, hn_ref, w1_ref, cw_ref, cb_ref, w2_ref, acc_ref, conv_fn):
    cols = slice(chunks[0] * FF_CHUNK, (chunks[-1] + 1) * FF_CHUNK)
    gcols = slice(D_FF + cols.start, D_FF + cols.stop)
    hn = hn_ref[...]
    a = _dot(hn, w1_ref[:, cols])
    gate_in = _dot(hn, w1_ref[:, gcols])
    cw = cw_ref[:, cols]
    a1, a2 = conv_fn(cols, a)
    ac = cb_ref[:, cols] + cw[0:1] * a2 + cw[1:2] * a1 + cw[2:3] * a
    hid = (ac * _sigmoid(ac)) * gate_in
    acc_ref[...] += _dot(hid.astype(BF16), w2_ref[cols, :])


def _ffn_post(x1_ref, acc_ref, p_ref, np_ref, wg_ref, wp_ref, nfin_ref, y_ref, final):
    x2 = x1_ref[...] + acc_ref[...]
    gate = _sigmoid(_dot(_rms(x2, np_ref[...]).astype(BF16), wg_ref[...]))
    x3 = x2 + gate * _dot(p_ref[...].astype(BF16), wp_ref[...])
    y_ref[...] = _rms(x3, nfin_ref[...]) if final else x3


def _split_chunks(n_parts):
    base, extra = divmod(N_FF_CHUNKS, n_parts)
    out, start = [], 0
    for i in range(n_parts):
        size = base + (1 if i < extra else 0)
        out.append(range(start, start + size))
        start += size
    return out


def _prompt_finish_kernel(pt_ref, x_ref, att_ref, gm_ref, p_ref, wo_ref, nf_ref, w1_ref, cw_ref,
                          cb_ref, w2_ref, np_ref, wg_ref, wp_ref, nfin_ref,
                          q_ref, kn_ref, vn_ref, lq1, lk1, lq2, lk2, sg_ref, ck_hbm, cv_hbm,
                          y_ref, cs_ref, atts_ref,
                          hn_ref, acc_ref, carry_ref, qbd_ref, m_ref, l_ref, accs_ref,
                          kbuf, vbuf, sem, *, layer, final, lam_init, n_tok, n_sub):
    npg = PAGES_PER_STEP
    x1_ref = y_ref
    s = pl.program_id(1)
    k = pl.program_id(2)
    tm = x_ref.shape[0]
    tile = pl.program_id(0) * pl.num_programs(1) + s
    n_tiles = pl.num_programs(0) * pl.num_programs(1)

    def page_copy(is_key, slot, i, page):
        src, dst = (ck_hbm, kbuf) if is_key else (cv_hbm, vbuf)
        return pltpu.make_async_copy(src.at[layer, page], dst.at[slot, i],
                                     sem.at[slot, 0 if is_key else 1])

    def start_pages(slot, row, sub):
        for i in range(npg):
            page = pt_ref[row, sub * npg + i]
            page_copy(True, slot, i, page).start(priority=0)
            page_copy(False, slot, i, page).start(priority=1)

    def wait_pages(slot):
        for i in range(npg):
            page_copy(True, slot, i, 0).wait()
            page_copy(False, slot, i, 0).wait()

    def conv_fn(cols, a):
        prev = carry_ref[:, cols]
        row = lax.broadcasted_iota(jnp.int32, a.shape, 0)
        a1 = jnp.where(row == 0, prev[1:2], pltpu.roll(a, 1, 0))
        a2 = jnp.where(row == 0, prev[0:1], jnp.where(row == 1, prev[1:2], pltpu.roll(a, 2, 0)))
        tail = a[tm - (CONV_W - 1):, :]
        carry_ref[:, cols] = tail
        cs_ref[:, cols] = tail
        return a1, a2

    @pl.when(jnp.logical_and(s == 0, k == 0))
    def _():
        carry_ref[...] = jnp.zeros(carry_ref.shape, F32)

    @pl.when(jnp.logical_and(tile == 0, k == 0))
    def _():
        start_pages(0, 0, 0)

    for sub, chunks in enumerate(_split_chunks(n_sub)):
        @pl.when(k == sub)
        def _(sub=sub, chunks=chunks):
            slot = sub % 2
            wait_pages(slot)
            if sub < n_sub - 1:
                start_pages(1 - slot, tile, sub + 1)
            else:
                @pl.when(tile + 1 < n_tiles)
                def _():
                    start_pages(1 - slot, tile + 1, 0)
            k_refs = [kbuf.at[slot, i] for i in range(npg)]
            v_refs = [vbuf.at[slot, i] for i in range(npg)]
            _sample_attn_step(sub == 0, sub == n_sub - 1, q_ref, kn_ref, vn_ref,
                              (lq1, lk1, lq2, lk2), sg_ref, k_refs, v_refs, atts_ref,
                              qbd_ref, m_ref, l_ref, accs_ref, lam_init, n_tok)
            if sub == 0:
                _ffn_pre(x_ref, att_ref, gm_ref, wo_ref, nf_ref, x1_ref, hn_ref, acc_ref)
            _ffn_chunk(chunks, hn_ref, w1_ref, cw_ref, cb_ref, w2_ref, acc_ref, conv_fn)
            if sub == n_sub - 1:
                _ffn_post(x1_ref, acc_ref, p_ref, np_ref, wg_ref, wp_ref, nfin_ref, y_ref, final)


def _sample_finish_kernel(x_ref, att_ref, gm_ref, p_ref, wo_ref, nf_ref, w1_ref, cw_ref,
                          cb_ref, w2_ref, np_ref, wg_ref, wp_ref, nfin_ref, st_ref,
                          y_ref, cs_ref, hn_ref, acc_ref, *, final, n_tok):
    x1_ref = y_ref

    def conv_fn(cols, a):
        m, w = a.shape
        a3 = a.reshape(m // n_tok, n_tok, w)
        prev = st_ref[:, :, cols]
        t = lax.broadcasted_iota(jnp.int32, a3.shape, 1)
        p0, p1 = prev[:, 0:1, :], prev[:, 1:2, :]
        a1 = jnp.where(t == 0, p1, pltpu.roll(a3, 1, 1))
        a2 = jnp.where(t == 0, p0, jnp.where(t == 1, p1, pltpu.roll(a3, 2, 1)))
        cs_ref[:, :, cols] = a3[:, n_tok - (CONV_W - 1):, :]
        return a1.reshape(m, w), a2.reshape(m, w)

    _ffn_pre(x_ref, att_ref, gm_ref, wo_ref, nf_ref, x1_ref, hn_ref, acc_ref)
    for chunks in _split_chunks(SAMPLE_FFN_SLABS):
        _ffn_chunk(chunks, hn_ref, w1_ref, cw_ref, cb_ref, w2_ref, acc_ref, conv_fn)
    _ffn_post(x1_ref, acc_ref, p_ref, np_ref, wg_ref, wp_ref, nfin_ref, y_ref, final)


def _finish_weight_specs(layer):
    return [
        _layer_spec((D_MODEL, D_MODEL), layer),
        _const_spec((1, D_MODEL)),
        _layer_spec((D_MODEL, 2 * D_FF), layer),
        _const_spec((CONV_W, D_FF)),
        _const_spec((1, D_FF)),
        _layer_spec((D_FF, D_MODEL), layer),
        _const_spec((1, D_MODEL)),
        _layer_spec((D_MODEL, D_MODEL), layer),
        _layer_spec((PLE_DIM, D_MODEL), layer),
        _const_spec((1, D_MODEL)),
    ]


def _finish_scratch(tm):
    return [
        pltpu.VMEM((tm, D_MODEL), BF16),
        pltpu.VMEM((tm, D_MODEL), F32),
    ]


def _prompt_finish(layer, x2d, att, gm, p2d, weights, batch, seq, final,
                   page_table, q_s, kn_s, vn_s, lams, sg, cache_kt, cache_vr, lam_init, n_tok):
    m = x2d.shape[0]
    tm = TM_FIN
    ns = seq // tm
    npg = PAGES_PER_STEP
    bs, n_pages = page_table.shape
    n_sub = n_pages // npg
    assert bs == batch * ns and n_pages == n_sub * npg
    assert n_sub <= N_FF_CHUNKS and n_sub % 2 == 0
    rows = ATT_HEADS * 2 * n_tok
    row = lambda b, s, k, pt: (b * ns + s, 0)
    fixed = lambda *_: (0, 0)
    in_specs = [
        pl.BlockSpec((tm, D_MODEL), row),
        pl.BlockSpec((tm, ATT_WIDTH), row),
        pl.BlockSpec((tm, GM_WIDTH), row),
        pl.BlockSpec((tm, PLE_DIM), row),
    ] + _finish_weight_specs(layer) + [
        pl.BlockSpec((n_tok, QK_COLS), row),
        pl.BlockSpec((n_tok, QK_COLS), row),
        pl.BlockSpec((n_tok, ATT_WIDTH), row),
    ] + [pl.BlockSpec((1, QK_DIM), fixed)] * 4 + [pl.BlockSpec((1, V_DIM), fixed)] + [
        pl.BlockSpec(memory_space=pl.ANY),
        pl.BlockSpec(memory_space=pl.ANY),
    ]
    page_buf = pltpu.VMEM((2, npg) + cache_kt.shape[2:], F32)
    grid_spec = pltpu.PrefetchScalarGridSpec(
        num_scalar_prefetch=1,
        grid=(batch, ns, n_sub),
        in_specs=in_specs,
        out_specs=[
            pl.BlockSpec((tm, D_MODEL), row),
            pl.BlockSpec((None, CONV_W - 1, D_FF), lambda b, s, k, pt: (b, 0, 0)),
            pl.BlockSpec((n_tok, ATT_WIDTH), row),
        ],
        scratch_shapes=_finish_scratch(tm) + [
            pltpu.VMEM((CONV_W - 1, D_FF), F32),
            pltpu.VMEM((rows, QK_COLS), BF16),
            pltpu.VMEM((rows, 1), F32),
            pltpu.VMEM((rows, 1), F32),
            pltpu.VMEM((rows, V_DIM), F32),
            page_buf,
            page_buf,
            pltpu.SemaphoreType.DMA((2, 2)),
        ],
    )
    return pl.pallas_call(
        functools.partial(_prompt_finish_kernel, layer=layer, final=final, lam_init=lam_init,
                          n_tok=n_tok, n_sub=n_sub),
        grid_spec=grid_spec,
        out_shape=[
            jax.ShapeDtypeStruct((m, D_MODEL), F32),
            jax.ShapeDtypeStruct((batch, CONV_W - 1, D_FF), F32),
            jax.ShapeDtypeStruct((bs * n_tok, ATT_WIDTH), F32),
        ],
        compiler_params=pltpu.CompilerParams(
            dimension_semantics=("arbitrary", "arbitrary", "arbitrary"),
            vmem_limit_bytes=VMEM_LIMIT),
        name="prompt_finish",
    )(page_table, x2d, att, gm, p2d, *weights, q_s, kn_s, vn_s, *lams, sg, cache_kt, cache_vr)


def _sample_finish(layer, x2d, att, gm, p2d, weights, state, n_tok, final):
    m = x2d.shape[0]
    row = lambda i: (0, 0)
    in_specs = [
        pl.BlockSpec((m, D_MODEL), row),
        pl.BlockSpec((m, ATT_WIDTH), row),
        pl.BlockSpec((m, GM_WIDTH), row),
        pl.BlockSpec((m, PLE_DIM), row),
    ] + _finish_weight_specs(layer) + [_const_spec(state.shape)]
    return pl.pallas_call(
        functools.partial(_sample_finish_kernel, final=final, n_tok=n_tok),
        grid=(1,),
        in_specs=in_specs,
        out_specs=[
            pl.BlockSpec((m, D_MODEL), row),
            pl.BlockSpec(state.shape, lambda i: (0, 0, 0)),
        ],
        out_shape=[
            jax.ShapeDtypeStruct((m, D_MODEL), F32),
            jax.ShapeDtypeStruct(state.shape, F32),
        ],
        scratch_shapes=_finish_scratch(m),
        compiler_params=pltpu.CompilerParams(
            dimension_semantics=("arbitrary",), vmem_limit_bytes=VMEM_LIMIT),
        name="sample_finish",
    )(x2d, att, gm, p2d, *weights, state)


def _rope_tables(pos):
    half = ROT_DIM // 2
    inv = ROPE_THETA ** (-jnp.arange(half, dtype=F32) * 2.0 / ROT_DIM)
    ang = pos.astype(F32)[:, None] * inv[None, :]
    cos, sin = jnp.cos(ang), jnp.sin(ang)
    n = pos.shape[0]
    rest = QK_DIM - ROT_DIM
    rc = jnp.concatenate([cos, cos, jnp.ones((n, rest), F32)], axis=1)
    ra = jnp.concatenate([-sin, jnp.zeros((n, half + rest), F32)], axis=1)
    rb = jnp.concatenate([jnp.zeros((n, half), F32), sin, jnp.zeros((n, rest), F32)], axis=1)
    rep = LANES // QK_DIM
    return tuple(jnp.tile(t, (1, rep)) for t in (rc, ra, rb))


def kernel(x_prompt, x_sample, cache_k, cache_v, state_ffn_conv, page_table, p_prompt, p_sample,
           norm_mix, w_in, lambda_q1, lambda_k1, lambda_q2, lambda_k2, subln_g, gm_norm_g,
           gm_w_s, gm_b_s, w_out, norm_ffn, w_ffn_in, conv_w, conv_b, w_ffn_out, norm_ple,
           w_ple_gate, w_ple_proj, norm_final):
    bp, seq, _ = x_prompt.shape
    bs, n_tok, _ = x_sample.shape
    n_phys = cache_k.shape[1]
    n_pages = page_table.shape[1]
    past_len = n_pages * PAGE_SIZE
    mp, ms = bp * seq, bs * n_tok

    w_in_b = w_in.astype(BF16)
    w_out_b = w_out.astype(BF16)
    w1_b = w_ffn_in.astype(BF16)
    w2_b = w_ffn_out.astype(BF16)
    wg_b = w_ple_gate.astype(BF16)
    wp_b = w_ple_proj.astype(BF16)

    rope_p = _rope_tables(jnp.arange(seq))
    rope_s = _rope_tables(past_len + jnp.arange(n_tok))
    rope_s = tuple(jnp.tile(t, (bs, 1)) for t in rope_s)
    gsum = jnp.kron(jnp.eye(GM_GROUPS, dtype=F32),
                    jnp.ones((GM_GROUP_DIM, GM_GROUP_DIM), F32)).astype(BF16)

    wpair = gm_w_s.reshape(DEPTH, GM_GROUPS // 2, 2, CHUNK, CHUNK).transpose(0, 1, 3, 2, 4)
    wpair = wpair.reshape(DEPTH, GM_GROUPS // 2, CHUNK, 2 * CHUNK).astype(BF16)
    bias_p = jnp.repeat(gm_b_s.transpose(0, 2, 1), GM_GROUP_DIM, axis=2)
    t_idx = jnp.arange(n_tok)[:, None]
    d_idx = jnp.arange(n_tok)[None, :]
    src = t_idx - d_idx
    w_small = jnp.tril(gm_w_s[:, :, :n_tok, :n_tok])
    coef = jnp.where(src >= 0, w_small[:, :, t_idx, jnp.maximum(src, 0)], 0.0)
    coef = jnp.repeat(coef.transpose(0, 3, 2, 1), GM_GROUP_DIM, axis=3)
    bias_s = jnp.repeat(gm_b_s[:, :, :n_tok].transpose(0, 2, 1), GM_GROUP_DIM, axis=2)

    cache_kt = cache_k.transpose(0, 1, 3, 4, 5, 2).reshape(DEPTH, n_phys, QK_COLS, PAGE_SIZE)
    cache_vr = cache_v.reshape(DEPTH, n_phys, PAGE_SIZE * ATT_HEADS, V_DIM)

    xp = x_prompt.reshape(mp, D_MODEL)
    xs = x_sample.reshape(ms, D_MODEL)
    pp = p_prompt.reshape(DEPTH, mp, PLE_DIM)
    ps = p_sample.reshape(DEPTH, ms, PLE_DIM)
    kall = jnp.zeros((DEPTH, bp, QK_COLS, seq), F32)
    vall = jnp.zeros((DEPTH, mp * ATT_HEADS, V_DIM), F32)
    nfin = norm_final[None, :]
    cp_l, ks_l, vs_l, cs_l, zs_l = [], [], [], [], []
    for i in range(DEPTH):
        lam_init = 0.8 - 0.6 * math.exp(-0.3 * i)
        final = i == DEPTH - 1
        lams = (lambda_q1[i][None], lambda_k1[i][None], lambda_q2[i][None], lambda_k2[i][None])
        sg = subln_g[i][None]
        gmix = norm_mix[i][None]
        gng = gm_norm_g[i][None]
        weights = (w_out_b, norm_ffn[i][None], w1_b, conv_w[i], conv_b[i][None], w2_b,
                   norm_ple[i][None], wg_b, wp_b, nfin)

        kall, vall, qb, kb, vb, gm_p = _prompt_mix(
            i, xp, gmix, w_in_b, rope_p, gsum, gng, wpair[i], bias_p[i], kall, vall, seq)
        att_p = _prompt_attn(qb, kb, vb, lams, sg, lam_init, bp, seq)
        q_s, k_s, v_s, z_s, gm_s = _sample_mix(
            i, xs, gmix, w_in_b, rope_s, gsum, gng, coef[i], bias_s[i])
        xp, cp, att_s = _prompt_finish(
            i, xp, att_p, gm_p, pp[i], weights, bp, seq, final,
            page_table, q_s, k_s, v_s, lams, sg, cache_kt, cache_vr, lam_init, n_tok)
        xs, cs = _sample_finish(i, xs, att_s, gm_s, ps[i], weights, state_ffn_conv[i], n_tok,
                                final)
        cp_l.append(cp)
        ks_l.append(k_s)
        vs_l.append(v_s)
        zs_l.append(z_s)
        cs_l.append(cs)

    return (
        xp.reshape(bp, seq, D_MODEL),
        xs.reshape(bs, n_tok, D_MODEL),
        kall.reshape(DEPTH, bp, ATT_HEADS, 2, QK_DIM, seq).transpose(0, 1, 5, 2, 3, 4),
        vall.reshape(DEPTH, bp, seq, ATT_HEADS, V_DIM),
        jnp.stack(cp_l),
        jnp.stack(ks_l).reshape(DEPTH, bs, n_tok, ATT_HEADS, 2, QK_DIM),
        jnp.stack(vs_l).reshape(DEPTH, bs, n_tok, ATT_HEADS, V_DIM),
        jnp.stack(cs_l),
        jnp.stack(zs_l).reshape(DEPTH, bs, n_tok, GM_WIDTH),
    )
```
